```python
import math
import jax, jax.numpy as jnp
from jax import lax
import numpy as np

D_MODEL = 1024
BATCH = 4
SEQ = 4096
DEPTH = 2
DEC_BATCH = 128
DEC_SEQ = 8
PAST_LEN = 2048
PAGE_SIZE = 128

R_HEADS = 4
R_DK = 128
R_DV = 256
R_CHUNK = 128
D_HEADS = 8
D_DK = 64
D_DV = 2 * D_DK
Q_BLOCK = 128
MEM_LEN = 256
M_HEADS = 4
M_DH = D_MODEL // M_HEADS
D_FF = -(-8 * D_MODEL // (3 * 256)) * 256
ALPHA = (2 * DEPTH) ** 0.25
BETA = (8 * DEPTH) ** -0.25
LN_EPS = 1e-5
RMS_EPS = 1e-6
NEG_INF = -1e30
R_QK = R_HEADS * R_DK
R_V = R_HEADS * R_DV
D_QK = D_HEADS * 2 * D_DK
D_V = D_HEADS * D_DV
IN_SIZES = [R_QK, R_QK, R_V, R_V, D_QK, D_QK, D_V, D_MODEL, D_MODEL]
IN_SPLITS = [sum(IN_SIZES[:i + 1]) for i in range(len(IN_SIZES) - 1)]
IN_WIDTH = sum(IN_SIZES)

kernel_name = 'retention_diffattn_alibi_deepnorm_memxattn_decoder'


def layer_norm(x, g, b):
    xf = x.astype(jnp.float32)
    mu = jnp.mean(xf, axis=-1, keepdims=True)
    var = jnp.mean(jnp.square(xf - mu), axis=-1, keepdims=True)
    return ((xf - mu) * lax.rsqrt(var + LN_EPS) * g + b).astype(x.dtype)


def head_rms(x, g):
    xf = x.astype(jnp.float32)
    return (xf * lax.rsqrt(jnp.mean(xf * xf, axis=-1, keepdims=True) + RMS_EPS) * g).astype(x.dtype)


def alibi_slopes(n):
    return 2.0 ** (-8.0 * jnp.arange(1, n + 1, dtype=jnp.float32) / n)


def retention_log_decay():
    return jnp.log(1.0 - 2.0 ** (-5.0 - jnp.arange(R_HEADS, dtype=jnp.float32)))


def retention_chunk(q, k, v, s0, log_gamma):
    L = q.shape[1]
    pos = jnp.arange(L, dtype=jnp.float32)
    diff = pos[:, None] - pos[None, :]
    causal = diff >= 0
    decay = jnp.where(causal[None], jnp.exp(log_gamma[:, None, None] * jnp.where(causal, diff, 0.0)[None]), 0.0)
    inner = jnp.einsum('blhk,bmhk->bhlm', q, k) * decay.astype(q.dtype)
    o_inner = jnp.einsum('bhlm,bmhv->blhv', inner, v)
    q_decay = jnp.exp(log_gamma[None, :] * (pos[:, None] + 1.0))
    o_cross = jnp.einsum('blhk,bhkv->blhv', q * q_decay[None, :, :, None].astype(q.dtype), s0)
    k_decay = jnp.exp(log_gamma[None, :] * (L - 1.0 - pos[:, None]))
    s_new = (jnp.exp(log_gamma * L)[None, :, None, None].astype(s0.dtype) * s0
             + jnp.einsum('blhk,blhv->bhkv', k * k_decay[None, :, :, None].astype(k.dtype), v))
    return o_inner + o_cross, s_new


def retention_prompt(q, k, v):
    b, s = q.shape[:2]
    nc = s // R_CHUNK
    log_gamma = retention_log_decay()

    def to_chunks(t):
        return t.reshape(b, nc, R_CHUNK, *t.shape[2:]).swapaxes(0, 1)

    def step(state, qkv):
        qc, kc, vc = qkv
        o, state = retention_chunk(qc, kc, vc, state, log_gamma)
        return state, o

    s0 = jnp.zeros((b, R_HEADS, R_DK, R_DV), q.dtype)
    s_fin, o = lax.scan(step, s0, (to_chunks(q), to_chunks(k), to_chunks(v)))
    return o.swapaxes(0, 1).reshape(b, s, R_HEADS, R_DV), s_fin


def diff_attn_prompt(q, k, v, lam):
    b, s = q.shape[:2]
    nb = s // Q_BLOCK
    slopes = alibi_slopes(D_HEADS)[None, :, None, None, None]
    kpos = jnp.arange(s)
    qb = q.reshape(b, nb, Q_BLOCK, D_HEADS, 2, D_DK).swapaxes(0, 1)

    def block(args):
        q_blk, i = args
        qpos = i * Q_BLOCK + jnp.arange(Q_BLOCK)
        dist = (qpos[:, None] - kpos[None, :]).astype(jnp.float32)
        sc = jnp.einsum('bqhcd,bkhcd->bhcqk', q_blk, k).astype(jnp.float32) * (D_DK ** -0.5)
        sc = jnp.where(dist >= 0, sc - slopes * dist, NEG_INF)
        p = jax.nn.softmax(sc, axis=-1)
        p = p[:, :, 0] - lam * p[:, :, 1]
        return jnp.einsum('bhqk,bkhv->bqhv', p.astype(v.dtype), v)

    o = lax.map(block, (qb, jnp.arange(nb)))
    return o.swapaxes(0, 1).reshape(b, s, D_HEADS, D_DV)


def diff_attn_sample(q, k_new, v_new, k_past, v_past, lam):
    t = q.shape[1]
    past = k_past.shape[1]
    slopes = alibi_slopes(D_HEADS)[None, :, None, None, None]
    qpos = past + jnp.arange(t)
    d_past = (qpos[:, None] - jnp.arange(past)[None, :]).astype(jnp.float32)
    d_new = (qpos[:, None] - qpos[None, :]).astype(jnp.float32)
    scale = D_DK ** -0.5
    sc_past = jnp.einsum('bqhcd,bkhcd->bhcqk', q, k_past).astype(jnp.float32) * scale - slopes * d_past
    sc_new = jnp.einsum('bqhcd,bkhcd->bhcqk', q, k_new).astype(jnp.float32) * scale - slopes * d_new
    sc_new = jnp.where(d_new >= 0, sc_new, NEG_INF)
    p = jax.nn.softmax(jnp.concatenate([sc_past, sc_new], axis=-1), axis=-1)
    p = (p[:, :, 0] - lam * p[:, :, 1]).astype(v_new.dtype)
    return (jnp.einsum('bhqk,bkhv->bqhv', p[..., :past], v_past)
            + jnp.einsum('bhqk,bkhv->bqhv', p[..., past:], v_new))


def mixer_sublayer(x, w_in, w_ret_o, w_diff_o, w_out, ret_g, diff_g, lam_p, lam_init, retention_fn, attention_fn):
    b, s, _ = x.shape
    rq, rk, rv, rg, dq, dk, dv, gr, gd = jnp.split(jnp.einsum('bsd,de->bse', x, w_in), IN_SPLITS, axis=-1)
    ro, r_state = retention_fn(rq.reshape(b, s, R_HEADS, R_DK),
                               rk.reshape(b, s, R_HEADS, R_DK) * (R_DK ** -0.5),
                               rv.reshape(b, s, R_HEADS, R_DV))
    ro = head_rms(ro, ret_g).reshape(b, s, R_V) * jax.nn.silu(rg)
    ret_branch = jnp.einsum('bse,ed->bsd', ro, w_ret_o)
    lam = jnp.exp(jnp.sum(lam_p[0] * lam_p[1])) - jnp.exp(jnp.sum(lam_p[2] * lam_p[3])) + lam_init
    dk = dk.reshape(b, s, D_HEADS, 2, D_DK)
    dv = dv.reshape(b, s, D_HEADS, D_DV)
    do = attention_fn(dq.reshape(b, s, D_HEADS, 2, D_DK), dk, dv, lam)
    do = head_rms(do, diff_g).reshape(b, s, D_V) * (1.0 - lam_init)
    diff_branch = jnp.einsum('bse,ed->bsd', do, w_diff_o)
    merged = jax.nn.sigmoid(gr) * ret_branch + jax.nn.sigmoid(gd) * diff_branch
    return jnp.einsum('bsd,de->bse', merged, w_out), dk, dv, r_state


def memory_kv(mem, w_mem_kv):
    b, m, _ = mem.shape
    mk, mv = jnp.split(jnp.einsum('bmd,de->bme', mem, w_mem_kv), 2, axis=-1)
    return mk.reshape(b, m, M_HEADS, M_DH), mv.reshape(b, m, M_HEADS, M_DH)


def memory_attention(x, mk, mv, w_q, w_o):
    b, s, _ = x.shape
    q = jnp.einsum('bsd,de->bse', x, w_q).reshape(b, s, M_HEADS, M_DH)
    sc = jnp.einsum('bshd,bmhd->bhsm', q, mk).astype(jnp.float32) * (M_DH ** -0.5)
    p = jax.nn.softmax(sc, axis=-1).astype(mv.dtype)
    o = jnp.einsum('bhsm,bmhd->bshd', p, mv).reshape(b, s, D_MODEL)
    return jnp.einsum('bsd,de->bse', o, w_o)


def swiglu(x, w_gu, w_down):
    g, u = jnp.split(jnp.einsum('bsd,df->bsf', x, w_gu), 2, axis=-1)
    return jnp.einsum('bsf,fd->bsd', jax.nn.silu(g) * u, w_down)


def finish_layer(x, mix, mk, mv, w_mem_q, w_mem_o, w_ffn_gu, w_ffn_down, ln_g, ln_b):
    x = layer_norm(ALPHA * x + mix, ln_g[0], ln_b[0])
    x = layer_norm(ALPHA * x + memory_attention(x, mk, mv, w_mem_q, w_mem_o), ln_g[1], ln_b[1])
    x = layer_norm(ALPHA * x + swiglu(x, w_ffn_gu, w_ffn_down), ln_g[2], ln_b[2])
    return x


def setup_inputs(seed: int = 0) -> dict:
    key = jax.random.key(seed)
    ks = jax.random.split(key, 32)
    f32 = jnp.float32

    def nrm(k, shape, scale):
        return jax.random.normal(k, shape, f32) * scale

    n_pages = PAST_LEN // PAGE_SIZE
    n_pool = (5 * DEC_BATCH * n_pages) // 4
    sd = D_MODEL ** -0.5
    page_table = jax.random.permutation(ks[0], n_pool)[:DEC_BATCH * n_pages].reshape(DEC_BATCH, n_pages).astype(jnp.int32)
    w_in = jnp.concatenate([
        nrm(ks[1], (DEPTH, D_MODEL, R_QK), sd),
        nrm(ks[2], (DEPTH, D_MODEL, R_QK), sd),
        nrm(ks[3], (DEPTH, D_MODEL, R_V), sd * BETA),
        nrm(ks[4], (DEPTH, D_MODEL, R_V), sd),
        nrm(ks[5], (DEPTH, D_MODEL, D_QK), sd),
        nrm(ks[6], (DEPTH, D_MODEL, D_QK), sd),
        nrm(ks[7], (DEPTH, D_MODEL, D_V), sd * BETA),
        nrm(ks[8], (DEPTH, D_MODEL, 2 * D_MODEL), sd),
    ], axis=-1)
    w_mem_kv = jnp.concatenate([nrm(ks[9], (DEPTH, D_MODEL, D_MODEL), sd),
                                nrm(ks[10], (DEPTH, D_MODEL, D_MODEL), sd * BETA)], axis=-1)
    return {
        'x_prompt': nrm(ks[11], (BATCH, SEQ, D_MODEL), 1.0),
        'x_sample': nrm(ks[12], (DEC_BATCH, DEC_SEQ, D_MODEL), 1.0),
        'mem_prompt': nrm(ks[13], (BATCH, MEM_LEN, D_MODEL), 1.0),
        'cache_k': nrm(ks[14], (DEPTH, n_pool, PAGE_SIZE, D_HEADS, 2, D_DK), 1.0),
        'cache_v': nrm(ks[15], (DEPTH, n_pool, PAGE_SIZE, D_HEADS, D_DV), 1.0),
        'state_ret': nrm(ks[16], (DEPTH, DEC_BATCH, R_HEADS, R_DK, R_DV), R_DK ** -0.5),
        'cache_mem_k': nrm(ks[17], (DEPTH, DEC_BATCH, MEM_LEN, M_HEADS, M_DH), 1.0),
        'cache_mem_v': nrm(ks[18], (DEPTH, DEC_BATCH, MEM_LEN, M_HEADS, M_DH), 1.0),
        'page_table': page_table,
        'w_in': w_in,
        'w_ret_o': nrm(ks[19], (DEPTH, R_V, D_MODEL), R_V ** -0.5 * BETA),
        'w_diff_o': nrm(ks[20], (DEPTH, D_V, D_MODEL), D_V ** -0.5 * BETA),
        'w_out': nrm(ks[21], (DEPTH, D_MODEL, D_MODEL), sd * BETA),
        'ret_norm_g': 1.0 + nrm(ks[22], (DEPTH, R_DV), 0.02),
        'diff_norm_g': 1.0 + nrm(ks[23], (DEPTH, D_DV), 0.02),
        'diff_lambda': nrm(ks[24], (DEPTH, 4, D_DK), 0.1),
        'w_mem_q': nrm(ks[25], (DEPTH, D_MODEL, D_MODEL), sd),
        'w_mem_kv': w_mem_kv,
        'w_mem_o': nrm(ks[26], (DEPTH, D_MODEL, D_MODEL), sd * BETA),
        'w_ffn_gu': nrm(ks[27], (DEPTH, D_MODEL, 2 * D_FF), sd),
        'w_ffn_down': nrm(ks[28], (DEPTH, D_FF, D_MODEL), D_FF ** -0.5 * BETA),
        'ln_g': 1.0 + nrm(ks[29], (DEPTH, 3, D_MODEL), 0.02),
        'ln_b': nrm(ks[30], (DEPTH, 3, D_MODEL), 0.02),
    }


def reference(x_prompt, x_sample, mem_prompt, cache_k, cache_v, state_ret, cache_mem_k, cache_mem_v, page_table,
              w_in, w_ret_o, w_diff_o, w_out, ret_norm_g, diff_norm_g, diff_lambda,
              w_mem_q, w_mem_kv, w_mem_o, w_ffn_gu, w_ffn_down, ln_g, ln_b):
    log_gamma = retention_log_decay()
    db, n_pages = page_table.shape
    past = n_pages * cache_k.shape[2]
    xp, xs = x_prompt, x_sample
    kp_l, vp_l, ks_l, vs_l, sp_l, ss_l, mkp_l, mvp_l = [], [], [], [], [], [], [], []
    for l in range(DEPTH):
        lam_init = 0.8 - 0.6 * math.exp(-0.3 * l)
        mk_p, mv_p = memory_kv(mem_prompt, w_mem_kv[l])
        mix_p, k_p, v_p, s_p = mixer_sublayer(xp, w_in[l], w_ret_o[l], w_diff_o[l], w_out[l], ret_norm_g[l],
                                              diff_norm_g[l], diff_lambda[l], lam_init,
                                              retention_prompt, diff_attn_prompt)
        xp = finish_layer(xp, mix_p, mk_p, mv_p, w_mem_q[l], w_mem_o[l], w_ffn_gu[l], w_ffn_down[l], ln_g[l], ln_b[l])
        k_past = cache_k[l][page_table].reshape(db, past, D_HEADS, 2, D_DK)
        v_past = cache_v[l][page_table].reshape(db, past, D_HEADS, D_DV)
        s_prev = state_ret[l]

        def ret_s(q, k, v, s_prev=s_prev):
            return retention_chunk(q, k, v, s_prev, log_gamma)

        def attn_s(q, k, v, lam, k_past=k_past, v_past=v_past):
            return diff_attn_sample(q, k, v, k_past, v_past, lam)

        mix_s, k_s, v_s, s_s = mixer_sublayer(xs, w_in[l], w_ret_o[l], w_diff_o[l], w_out[l], ret_norm_g[l],
                                              diff_norm_g[l], diff_lambda[l], lam_init, ret_s, attn_s)
        xs = finish_layer(xs, mix_s, cache_mem_k[l], cache_mem_v[l], w_mem_q[l], w_mem_o[l], w_ffn_gu[l],
                          w_ffn_down[l], ln_g[l], ln_b[l])
        kp_l.append(k_p); vp_l.append(v_p); ks_l.append(k_s); vs_l.append(v_s)
        sp_l.append(s_p); ss_l.append(s_s); mkp_l.append(mk_p); mvp_l.append(mv_p)
    return (xp, xs, jnp.stack(kp_l), jnp.stack(vp_l), jnp.stack(ks_l), jnp.stack(vs_l),
            jnp.stack(sp_l), jnp.stack(ss_l), jnp.stack(mkp_l), jnp.stack(mvp_l))
```

```python
import functools
import math

import jax
import jax.numpy as jnp
from jax import lax
from jax.experimental import pallas as pl
from jax.experimental.pallas import tpu as pltpu

F32 = jnp.float32
BF16 = jnp.bfloat16

D_MODEL = 1024
R_HEADS, R_DK, R_DV = 4, 128, 256
D_HEADS, D_DK = 8, 64
D_DV = 2 * D_DK
M_HEADS = 4
M_DH = D_MODEL // M_HEADS
D_FF = -(-8 * D_MODEL // (3 * 256)) * 256
LN_EPS = 1e-5
RMS_EPS = 1e-6
NEG_INF = -1e30

R_QK = R_HEADS * R_DK
R_V = R_HEADS * R_DV
D_QK = D_HEADS * 2 * D_DK
D_V = D_HEADS * D_DV
OFF_RQ = 0
OFF_RK = OFF_RQ + R_QK
OFF_RV = OFF_RK + R_QK
OFF_RG = OFF_RV + R_V
OFF_DQ = OFF_RG + R_V
OFF_DK = OFF_DQ + D_QK
OFF_DV = OFF_DK + D_QK
OFF_GR = OFF_DV + D_V
OFF_GD = OFF_GR + D_MODEL
IN_WIDTH = OFF_GD + D_MODEL

LOG_GAMMAS = tuple(math.log(1.0 - 2.0 ** (-5.0 - h)) for h in range(R_HEADS))
ALIBI_SLOPES = tuple(2.0 ** (-8.0 * (h + 1) / D_HEADS) for h in range(D_HEADS))

VMEM_LIMIT_BYTES = 56 * 1024 * 1024


def _params(n_grid_dims):
    return pltpu.CompilerParams(dimension_semantics=("arbitrary",) * n_grid_dims,
                                vmem_limit_bytes=VMEM_LIMIT_BYTES)


def _select_const(idx, values):
    out = jnp.float32(values[-1])
    for i in range(len(values) - 2, -1, -1):
        out = jnp.where(idx == i, jnp.float32(values[i]), out)
    return out


def _dot(a, b):
    return jnp.dot(a, b, preferred_element_type=F32)


def _dot_nt(a, b):
    return lax.dot_general(a, b, (((1,), (1,)), ((), ())), preferred_element_type=F32)


def _dot_tn(a, b):
    return lax.dot_general(a, b, (((0,), (0,)), ((), ())), preferred_element_type=F32)


def _layer_norm(y, g, b):
    mu = jnp.mean(y, axis=-1, keepdims=True)
    yc = y - mu
    var = jnp.mean(yc * yc, axis=-1, keepdims=True)
    return yc * lax.rsqrt(var + LN_EPS) * g + b


def _head_rms(o, g):
    return o * lax.rsqrt(jnp.mean(o * o, axis=-1, keepdims=True) + RMS_EPS) * g


def _diff_lambda(lam_ref, lam_init):
    lp = lam_ref[...]
    a = jnp.sum(lp[0:1, :] * lp[1:2, :], axis=-1, keepdims=True)
    b = jnp.sum(lp[2:3, :] * lp[3:4, :], axis=-1, keepdims=True)
    return jnp.exp(a) - jnp.exp(b) + lam_init


def _mm_body(x_ref, w_ref, o_ref, xb_ref):
    @pl.when(pl.program_id(1) == 0)
    def _():
        xb_ref[...] = x_ref[...].astype(BF16)

    o_ref[...] = _dot(xb_ref[...], w_ref[...])


def _matmul(x, w, layer, col0, ncols, name):
    m, k = x.shape
    tm = min(m, 1024)
    tn = min(ncols, 512)
    cb = col0 // tn
    return pl.pallas_call(
        _mm_body,
        grid=(m // tm, ncols // tn),
        in_specs=[pl.BlockSpec((tm, k), lambda i, j: (i, 0)),
                  pl.BlockSpec((None, k, tn), lambda i, j: (layer, 0, j + cb))],
        out_specs=pl.BlockSpec((tm, tn), lambda i, j: (i, j)),
        out_shape=jax.ShapeDtypeStruct((m, ncols), F32),
        scratch_shapes=[pltpu.VMEM((tm, k), BF16)],
        compiler_params=_params(2),
        name=name,
    )(x, w)


def _ret_body(*refs, chunk, nb, has_s0):
    if has_s0:
        q_ref, k_ref, v_ref, rg_ref, g_ref, s0_ref, o_ref, s_out_ref, s_scr = refs
    else:
        q_ref, k_ref, v_ref, rg_ref, g_ref, o_ref, s_out_ref, s_scr = refs
    h = pl.program_id(1)
    c = pl.program_id(2)
    lg = _select_const(h, LOG_GAMMAS)

    @pl.when(c == 0)
    def _():
        if has_s0:
            s_scr[...] = s0_ref[:, 0]
        else:
            s_scr[...] = jnp.zeros_like(s_scr)

    ii = lax.broadcasted_iota(jnp.int32, (chunk, chunk), 0)
    jj = lax.broadcasted_iota(jnp.int32, (chunk, chunk), 1)
    causal = ii >= jj
    dist = jnp.where(causal, (ii - jj).astype(F32), 0.0)
    decay = jnp.where(causal, jnp.exp(lg * dist), 0.0)
    pos = lax.broadcasted_iota(jnp.int32, (chunk, 1), 0).astype(F32)
    q_decay = jnp.exp(lg * (pos + 1.0))
    k_decay = jnp.exp(lg * (chunk - 1.0 - pos))
    s_decay = jnp.exp(jnp.full((1, 1), chunk, F32) * lg)
    g = g_ref[...]

    for n in range(nb):
        rows = slice(n * chunk, (n + 1) * chunk)
        q = q_ref[rows, :]
        k = k_ref[rows, :] * (R_DK ** -0.5)
        vb = v_ref[rows, :].astype(BF16)
        s_prev = s_scr[n]
        inner = _dot_nt(q.astype(BF16), k.astype(BF16)) * decay
        o = _dot(inner.astype(BF16), vb) + _dot((q * q_decay).astype(BF16), s_prev.astype(BF16))
        s_scr[n] = s_decay * s_prev + _dot_tn((k * k_decay).astype(BF16), vb)
        rg = rg_ref[rows, :]
        o_ref[rows, :] = (_head_rms(o, g) * (rg * jax.nn.sigmoid(rg))).astype(o_ref.dtype)

    @pl.when(c == pl.num_programs(2) - 1)
    def _():
        s_out_ref[:, 0] = s_scr[...]


def _retention(pa, ret_g, s0, *, layer, batch, seq, chunk, nb, out_dtype, name):
    nc = seq // chunk
    rows = nb * chunk
    has_s0 = s0 is not None

    def row_idx(b, c):
        return b * nc + c

    in_specs = [
        pl.BlockSpec((rows, R_DK), lambda b, h, c: (row_idx(b, c), OFF_RQ // R_DK + h)),
        pl.BlockSpec((rows, R_DK), lambda b, h, c: (row_idx(b, c), OFF_RK // R_DK + h)),
        pl.BlockSpec((rows, R_DV), lambda b, h, c: (row_idx(b, c), OFF_RV // R_DV + h)),
        pl.BlockSpec((rows, R_DV), lambda b, h, c: (row_idx(b, c), OFF_RG // R_DV + h)),
        pl.BlockSpec((1, R_DV), lambda b, h, c: (0, 0)),
    ]
    args = [pa, pa, pa, pa, ret_g]
    state_spec = pl.BlockSpec((nb, 1, R_DK, R_DV), lambda b, h, c: (b, h, 0, 0))
    if has_s0:
        in_specs.append(pl.BlockSpec((None, nb, 1, R_DK, R_DV), lambda b, h, c: (layer, b, h, 0, 0)))
        args.append(s0)
    return pl.pallas_call(
        functools.partial(_ret_body, chunk=chunk, nb=nb, has_s0=has_s0),
        grid=(batch // nb, R_HEADS, nc),
        in_specs=in_specs,
        out_specs=[pl.BlockSpec((rows, R_DV), lambda b, h, c: (row_idx(b, c), h)), state_spec],
        out_shape=[jax.ShapeDtypeStruct((batch * seq, R_V), out_dtype),
                   jax.ShapeDtypeStruct((batch, R_HEADS, R_DK, R_DV), F32)],
        scratch_shapes=[pltpu.VMEM((nb, R_DK, R_DV), F32)],
        compiler_params=_params(3),
        name=name,
    )(*args)


def _dattn_p_body(qi_tab, ki_tab, q_ref, k_ref, v_ref, lam_ref, g_ref, o_ref, qs, m_s, l_s, acc, *, tq, lam_init):
    h = pl.program_id(1)
    t = pl.program_id(2)
    qi = qi_tab[t]
    ki = ki_tab[t]
    slope = _select_const(h, ALIBI_SLOPES)

    @pl.when(ki == 0)
    def _():
        q = q_ref[...] * (D_DK ** -0.5)
        lane = lax.broadcasted_iota(jnp.int32, q.shape, 1)
        qs[0:tq, :] = jnp.where(lane < D_DK, q, 0.0).astype(BF16)
        qs[tq:2 * tq, :] = jnp.where(lane >= D_DK, q, 0.0).astype(BF16)
        m_s[...] = jnp.full_like(m_s, NEG_INF)
        l_s[...] = jnp.zeros_like(l_s)
        acc[...] = jnp.zeros_like(acc)

    def step(masked):
        s = _dot_nt(qs[...], k_ref[...].astype(BF16))
        col = lax.broadcasted_iota(jnp.int32, (1, tq), 1)
        s = s + slope * (col + (ki - qi) * tq).astype(F32)
        if masked:
            row = lax.broadcasted_iota(jnp.int32, s.shape, 0)
            row = jnp.where(row >= tq, row - tq, row)
            s = jnp.where(row >= col, s, NEG_INF)
        m_prev = m_s[...]
        m_new = jnp.maximum(m_prev, jnp.max(s, axis=-1, keepdims=True))
        alpha = jnp.exp(m_prev - m_new)
        p = jnp.exp(s - m_new)
        l_s[...] = alpha * l_s[...] + jnp.sum(p, axis=-1, keepdims=True)
        acc[...] = alpha * acc[...] + _dot(p.astype(BF16), v_ref[...].astype(BF16))
        m_s[...] = m_new

    @pl.when(ki < qi)
    def _():
        step(False)

    @pl.when(ki == qi)
    def _():
        step(True)
        lam = _diff_lambda(lam_ref, lam_init)
        o = acc[...] / l_s[...]
        o = o[0:tq, :] - lam * o[tq:2 * tq, :]
        o_ref[...] = (_head_rms(o, g_ref[...]) * (1.0 - lam_init)).astype(o_ref.dtype)


def _diff_attn_prompt(pa, kk, vv, lam_p, diff_g, *, batch, seq, lam_init, name):
    tq = min(seq, 512)
    nq = seq // tq
    pairs = [(qi, ki) for qi in range(nq) for ki in range(qi + 1)]
    qi_tab = jnp.asarray([p[0] for p in pairs], jnp.int32)
    ki_tab = jnp.asarray([p[1] for p in pairs], jnp.int32)
    grid_spec = pltpu.PrefetchScalarGridSpec(
        num_scalar_prefetch=2,
        grid=(batch, D_HEADS, len(pairs)),
        in_specs=[
            pl.BlockSpec((tq, D_DV), lambda b, h, t, qt, kt: (b * nq + qt[t], OFF_DQ // D_DV + h)),
            pl.BlockSpec((tq, D_DV), lambda b, h, t, qt, kt: (b * nq + kt[t], h)),
            pl.BlockSpec((tq, D_DV), lambda b, h, t, qt, kt: (b * nq + kt[t], h)),
            pl.BlockSpec((4, D_DK), lambda b, h, t, qt, kt: (0, 0)),
            pl.BlockSpec((1, D_DV), lambda b, h, t, qt, kt: (0, 0)),
        ],
        out_specs=pl.BlockSpec((tq, D_DV), lambda b, h, t, qt, kt: (b * nq + qt[t], h)),
        scratch_shapes=[pltpu.VMEM((2 * tq, D_DV), BF16), pltpu.VMEM((2 * tq, 1), F32),
                        pltpu.VMEM((2 * tq, 1), F32), pltpu.VMEM((2 * tq, D_DV), F32)],
    )
    return pl.pallas_call(
        functools.partial(_dattn_p_body, tq=tq, lam_init=lam_init),
        grid_spec=grid_spec,
        out_shape=jax.ShapeDtypeStruct((batch * seq, D_V), BF16),
        compiler_params=_params(3),
        name=name,
    )(qi_tab, ki_tab, pa, kk, vv, lam_p, diff_g)


def _block_diag_rows(q, n_groups, group_width):
    t = q.shape[0]
    qt = jnp.concatenate([q] * n_groups, axis=0)
    rg = lax.broadcasted_iota(jnp.int32, qt.shape, 0) // t
    cg = lax.broadcasted_iota(jnp.int32, qt.shape, 1) // group_width
    return jnp.where(rg == cg, qt, 0.0)


def _dattn_s_body(pt_ref, q_ref, kn_ref, vn_ref, *rest, n_pages, page, t_new, lam_init):
    k_refs = rest[:n_pages]
    v_refs = rest[n_pages:2 * n_pages]
    lam_ref, g_ref, o_ref = rest[2 * n_pages:]
    past = n_pages * page
    n_rows = 2 * D_HEADS * t_new

    qbd = _block_diag_rows(q_ref[...] * (D_DK ** -0.5), 2 * D_HEADS, D_DK).astype(BF16)
    r = lax.broadcasted_iota(jnp.int32, (n_rows, 1), 0)
    slope = jnp.exp2(-8.0 * ((r // (2 * t_new)) + 1).astype(F32) / D_HEADS)
    qpos = past + (r % t_new)

    s_past = jnp.concatenate([_dot(qbd, k_refs[j][...].astype(BF16)) for j in range(n_pages)], axis=1)
    kpos = lax.broadcasted_iota(jnp.int32, (1, past), 1)
    s_past = s_past - slope * (qpos - kpos).astype(F32)

    pad = jnp.zeros((page - t_new, D_QK), F32)
    kn = jnp.concatenate([kn_ref[...], pad], axis=0).astype(BF16)
    vn = jnp.concatenate([vn_ref[...], pad], axis=0).astype(BF16)
    d_new = qpos - (past + lax.broadcasted_iota(jnp.int32, (1, page), 1))
    s_new = jnp.where(d_new >= 0, _dot_nt(qbd, kn) - slope * d_new.astype(F32), NEG_INF)

    m = jnp.maximum(jnp.max(s_past, axis=-1, keepdims=True), jnp.max(s_new, axis=-1, keepdims=True))
    p_past = jnp.exp(s_past - m)
    p_new = jnp.exp(s_new - m)
    denom = jnp.sum(p_past, axis=-1, keepdims=True) + jnp.sum(p_new, axis=-1, keepdims=True)
    acc = _dot(p_new.astype(BF16), vn)
    for j in range(n_pages):
        v_page = jnp.concatenate([v_refs[j][pl.ds(h, page, stride=D_HEADS), :] for h in range(D_HEADS)], axis=1)
        acc = acc + _dot(p_past[:, j * page:(j + 1) * page].astype(BF16), v_page.astype(BF16))
    o = acc / denom

    lam = _diff_lambda(lam_ref, lam_init)
    g = g_ref[...]
    heads = []
    for h in range(D_HEADS):
        cols = slice(h * D_DV, (h + 1) * D_DV)
        o1 = o[(2 * h) * t_new:(2 * h + 1) * t_new, cols]
        o2 = o[(2 * h + 1) * t_new:(2 * h + 2) * t_new, cols]
        heads.append(_head_rms(o1 - lam * o2, g) * (1.0 - lam_init))
    o_ref[...] = jnp.concatenate(heads, axis=1)


def _diff_attn_sample(pa, kk, vv, cache_k, cache_v, page_table, lam_p, diff_g, *, layer, t_new, lam_init, name):
    db, n_pages = page_table.shape
    page = cache_k.shape[3]
    pt = page_table.reshape(-1)

    def page_spec(j, shape):
        return pl.BlockSpec((None, None) + shape, lambda b, pt_ref: (layer, pt_ref[b * n_pages + j], 0, 0))

    grid_spec = pltpu.PrefetchScalarGridSpec(
        num_scalar_prefetch=1,
        grid=(db,),
        in_specs=[pl.BlockSpec((t_new, D_QK), lambda b, pt_ref: (b, OFF_DQ // D_QK)),
                  pl.BlockSpec((t_new, D_QK), lambda b, pt_ref: (b, 0)),
                  pl.BlockSpec((t_new, D_V), lambda b, pt_ref: (b, 0))]
        + [page_spec(j, (D_QK, page)) for j in range(n_pages)]
        + [page_spec(j, (page * D_HEADS, D_DV)) for j in range(n_pages)]
        + [pl.BlockSpec((4, D_DK), lambda b, pt_ref: (0, 0)),
           pl.BlockSpec((1, D_DV), lambda b, pt_ref: (0, 0))],
        out_specs=pl.BlockSpec((t_new, D_V), lambda b, pt_ref: (b, 0)),
    )
    return pl.pallas_call(
        functools.partial(_dattn_s_body, n_pages=n_pages, page=page, t_new=t_new, lam_init=lam_init),
        grid_spec=grid_spec,
        out_shape=jax.ShapeDtypeStruct((db * t_new, D_V), F32),
        compiler_params=_params(1),
        name=name,
    )(pt, pa, kk, vv, *([cache_k] * n_pages), *([cache_v] * n_pages), lam_p, diff_g)


def _mix_body(ro_ref, do_ref, gr_ref, gd_ref, x_ref, wr_ref, wd_ref, wo_ref, g_ref, b_ref, o_ref, *, alpha):
    ret_branch = _dot(ro_ref[...].astype(BF16), wr_ref[...])
    diff_branch = _dot(do_ref[...].astype(BF16), wd_ref[...])
    merged = jax.nn.sigmoid(gr_ref[...]) * ret_branch + jax.nn.sigmoid(gd_ref[...]) * diff_branch
    mix = _dot(merged.astype(BF16), wo_ref[...])
    o_ref[...] = _layer_norm(alpha * x_ref[...] + mix, g_ref[...], b_ref[...])


def _mixer_out(ro, do, gates, x, w_ret_o, w_diff_o, w_out, ln_g, ln_b, *, layer, alpha, name):
    m = x.shape[0]
    tm = min(m, 512)
    row = lambda i: (i, 0)
    wspec = pl.BlockSpec((None, D_MODEL, D_MODEL), lambda i: (layer, 0, 0))
    lnspec = pl.BlockSpec((None, None, 1, D_MODEL), lambda i: (layer, 0, 0, 0))
    return pl.pallas_call(
        functools.partial(_mix_body, alpha=alpha),
        grid=(m // tm,),
        in_specs=[pl.BlockSpec((tm, R_V), row), pl.BlockSpec((tm, D_V), row),
                  pl.BlockSpec((tm, D_MODEL), lambda i: (i, 0)), pl.BlockSpec((tm, D_MODEL), lambda i: (i, 1)),
                  pl.BlockSpec((tm, D_MODEL), row), wspec, wspec, wspec, lnspec, lnspec],
        out_specs=pl.BlockSpec((tm, D_MODEL), row),
        out_shape=jax.ShapeDtypeStruct((m, D_MODEL), F32),
        compiler_params=_params(1),
        name=name,
    )(ro, do, gates, gates, x, w_ret_o, w_diff_o, w_out, ln_g, ln_b)


def _softmax_rows(s):
    m = jnp.max(s, axis=-1, keepdims=True)
    e = jnp.exp(s - m)
    return e / jnp.sum(e, axis=-1, keepdims=True)


def _mem_p_body(x_ref, mk_ref, mv_ref, wq_ref, wo_ref, g_ref, b_ref, o_ref, *, alpha):
    x = x_ref[...]
    q = _dot(x.astype(BF16), wq_ref[...])
    heads = []
    for h in range(M_HEADS):
        cols = slice(h * M_DH, (h + 1) * M_DH)
        s = _dot_nt(q[:, cols].astype(BF16), mk_ref[:, cols].astype(BF16)) * (M_DH ** -0.5)
        heads.append(_dot(_softmax_rows(s).astype(BF16), mv_ref[:, cols].astype(BF16)))
    o = jnp.concatenate(heads, axis=1)
    att = _dot(o.astype(BF16), wo_ref[...])
    o_ref[...] = _layer_norm(alpha * x + att, g_ref[...], b_ref[...])


def _mem_attn_prompt(x, mk, mv, w_q, w_o, ln_g, ln_b, *, layer, seq, alpha, name):
    m = x.shape[0]
    mem_len = mk.shape[0] // (m // seq)
    tm = min(seq, 512)
    per_batch = seq // tm
    row = lambda i: (i, 0)
    wspec = pl.BlockSpec((None, D_MODEL, D_MODEL), lambda i: (layer, 0, 0))
    lnspec = pl.BlockSpec((None, None, 1, D_MODEL), lambda i: (layer, 1, 0, 0))
    mspec = pl.BlockSpec((mem_len, D_MODEL), lambda i: (i // per_batch, 0))
    return pl.pallas_call(
        functools.partial(_mem_p_body, alpha=alpha),
        grid=(m // tm,),
        in_specs=[pl.BlockSpec((tm, D_MODEL), row), mspec, mspec, wspec, wspec, lnspec, lnspec],
        out_specs=pl.BlockSpec((tm, D_MODEL), row),
        out_shape=jax.ShapeDtypeStruct((m, D_MODEL), F32),
        compiler_params=_params(1),
        name=name,
    )(x, mk, mv, w_q, w_o, ln_g, ln_b)


def _mem_s_body(x_ref, mk_ref, mv_ref, wq_ref, wo_ref, g_ref, b_ref, o_ref, *, alpha, n_samples, t_new):
    x = x_ref[...]
    q = _dot(x.astype(BF16), wq_ref[...])
    outs = []
    mem_len = mk_ref.shape[1] // (D_MODEL // 128)
    halves = M_DH // 128

    def mem_rows(ref, n):
        return jnp.concatenate(
            [ref[n, pl.ds(j * M_HEADS + h, mem_len, stride=M_HEADS * halves), :]
             for h in range(M_HEADS) for j in range(halves)], axis=1).astype(BF16)

    for n in range(n_samples):
        qbd = _block_diag_rows(q[n * t_new:(n + 1) * t_new, :], M_HEADS, M_DH).astype(BF16)
        s = _dot_nt(qbd, mem_rows(mk_ref, n)) * (M_DH ** -0.5)
        o = _dot(_softmax_rows(s).astype(BF16), mem_rows(mv_ref, n))
        outs.append(jnp.concatenate(
            [o[h * t_new:(h + 1) * t_new, h * M_DH:(h + 1) * M_DH] for h in range(M_HEADS)], axis=1))
    o = jnp.concatenate(outs, axis=0)
    att = _dot(o.astype(BF16), wo_ref[...])
    o_ref[...] = _layer_norm(alpha * x + att, g_ref[...], b_ref[...])


def _mem_attn_sample(x, cache_mk, cache_mv, w_q, w_o, ln_g, ln_b, *, layer, t_new, alpha, name):
    m = x.shape[0]
    db, mem_rows = cache_mk.shape[1:3]
    ns = min(db, 4)
    rows = ns * t_new
    row = lambda i: (i, 0)
    wspec = pl.BlockSpec((None, D_MODEL, D_MODEL), lambda i: (layer, 0, 0))
    lnspec = pl.BlockSpec((None, None, 1, D_MODEL), lambda i: (layer, 1, 0, 0))
    mspec = pl.BlockSpec((None, ns, mem_rows, 128), lambda i: (layer, i, 0, 0))
    return pl.pallas_call(
        functools.partial(_mem_s_body, alpha=alpha, n_samples=ns, t_new=t_new),
        grid=(db // ns,),
        in_specs=[pl.BlockSpec((rows, D_MODEL), row), mspec, mspec, wspec, wspec, lnspec, lnspec],
        out_specs=pl.BlockSpec((rows, D_MODEL), row),
        out_shape=jax.ShapeDtypeStruct((m, D_MODEL), F32),
        compiler_params=_params(1),
        name=name,
    )(x, cache_mk, cache_mv, w_q, w_o, ln_g, ln_b)


FF_CHUNK = 256


def _ffn_body(x_ref, wgu_ref, wd_ref, g_ref, b_ref, o_ref, *, alpha):
    x = x_ref[...]
    xb = x.astype(BF16)
    y = alpha * x
    for c in range(D_FF // FF_CHUNK):
        gate = _dot(xb, wgu_ref[:, c * FF_CHUNK:(c + 1) * FF_CHUNK])
        up = _dot(xb, wgu_ref[:, D_FF + c * FF_CHUNK:D_FF + (c + 1) * FF_CHUNK])
        hidden = (gate * jax.nn.sigmoid(gate) * up).astype(BF16)
        y = y + _dot(hidden, wd_ref[c * FF_CHUNK:(c + 1) * FF_CHUNK, :])
    o_ref[...] = _layer_norm(y, g_ref[...], b_ref[...])


def _ffn(x, w_gu, w_down, ln_g, ln_b, *, layer, alpha, name):
    m = x.shape[0]
    tm = min(m, 512)
    row = lambda i: (i, 0)
    lnspec = pl.BlockSpec((None, None, 1, D_MODEL), lambda i: (layer, 2, 0, 0))
    return pl.pallas_call(
        functools.partial(_ffn_body, alpha=alpha),
        grid=(m // tm,),
        in_specs=[pl.BlockSpec((tm, D_MODEL), row),
                  pl.BlockSpec((None, D_MODEL, 2 * D_FF), lambda i: (layer, 0, 0)),
                  pl.BlockSpec((None, D_FF, D_MODEL), lambda i: (layer, 0, 0)),
                  lnspec, lnspec],
        out_specs=pl.BlockSpec((tm, D_MODEL), row),
        out_shape=jax.ShapeDtypeStruct((m, D_MODEL), F32),
        compiler_params=_params(1),
        name=name,
    )(x, w_gu, w_down, ln_g, ln_b)


def kernel(x_prompt, x_sample, mem_prompt, cache_k, cache_v, state_ret, cache_mem_k, cache_mem_v, page_table, w_in, w_ret_o, w_diff_o, w_out, ret_norm_g, diff_norm_g, diff_lambda, w_mem_q, w_mem_kv, w_mem_o, w_ffn_gu, w_ffn_down, ln_g, ln_b):
    depth = w_in.shape[0]
    batch, seq, _ = x_prompt.shape
    db, t_new, _ = x_sample.shape
    mem_len = mem_prompt.shape[1]
    n_pool, page = cache_k.shape[1:3]
    alpha = (2 * depth) ** 0.25

    w_in, w_ret_o, w_diff_o, w_out, w_mem_q, w_mem_kv, w_mem_o, w_ffn_gu, w_ffn_down = (
        w.astype(BF16) for w in (w_in, w_ret_o, w_diff_o, w_out, w_mem_q, w_mem_kv, w_mem_o, w_ffn_gu, w_ffn_down))
    cache_k = jnp.transpose(cache_k, (0, 1, 3, 4, 5, 2)).reshape(depth, n_pool, D_QK, page)
    cache_v = cache_v.reshape(depth, n_pool, page * D_HEADS, D_DV)

    def mem_view(c):
        c = c.reshape(depth, db, mem_len, M_HEADS, M_DH // 128, 128)
        return jnp.transpose(c, (0, 1, 2, 4, 3, 5)).reshape(depth, db, mem_len * (D_MODEL // 128), 128)

    cache_mem_k = mem_view(cache_mem_k)
    cache_mem_v = mem_view(cache_mem_v)
    ln_g = ln_g.reshape(depth, 3, 1, D_MODEL)
    ln_b = ln_b.reshape(depth, 3, 1, D_MODEL)
    mem = mem_prompt.reshape(batch * mem_len, D_MODEL)
    xp = x_prompt.reshape(batch * seq, D_MODEL)
    xs = x_sample.reshape(db * t_new, D_MODEL)

    outs = [[] for _ in range(8)]
    for l in range(depth):
        lam_init = 0.8 - 0.6 * math.exp(-0.3 * l)
        ret_g = ret_norm_g[l].reshape(1, R_DV)
        diff_g = diff_norm_g[l].reshape(1, D_DV)
        lam_p = diff_lambda[l]

        mk_p = _matmul(mem, w_mem_kv, l, 0, D_MODEL, f"mem_k_{l}")
        mv_p = _matmul(mem, w_mem_kv, l, D_MODEL, D_MODEL, f"mem_v_{l}")

        def mixer(x, tag, retention_fn, attention_fn):
            pa = _matmul(x, w_in, l, 0, OFF_DK, f"proj_a_{tag}{l}")
            kk = _matmul(x, w_in, l, OFF_DK, D_QK, f"proj_k_{tag}{l}")
            vv = _matmul(x, w_in, l, OFF_DV, D_V, f"proj_v_{tag}{l}")
            gates = _matmul(x, w_in, l, OFF_GR, 2 * D_MODEL, f"proj_g_{tag}{l}")
            ro, r_state = retention_fn(pa)
            do = attention_fn(pa, kk, vv)
            x1 = _mixer_out(ro, do, gates, x, w_ret_o, w_diff_o, w_out, ln_g, ln_b, layer=l, alpha=alpha,
                            name=f"mixer_out_{tag}{l}")
            return x1, kk, vv, r_state

        xp, k_p, v_p, s_p = mixer(
            xp, "p",
            lambda pa: _retention(pa, ret_g, None, layer=l, batch=batch, seq=seq, chunk=min(seq, 256), nb=1,
                                  out_dtype=BF16, name=f"retention_p{l}"),
            lambda pa, kk, vv: _diff_attn_prompt(pa, kk, vv, lam_p, diff_g, batch=batch, seq=seq,
                                                 lam_init=lam_init, name=f"diff_attn_p{l}"))
        xp = _mem_attn_prompt(xp, mk_p, mv_p, w_mem_q, w_mem_o, ln_g, ln_b, layer=l, seq=seq, alpha=alpha,
                              name=f"mem_attn_p{l}")
        xp = _ffn(xp, w_ffn_gu, w_ffn_down, ln_g, ln_b, layer=l, alpha=alpha, name=f"ffn_p{l}")

        xs, k_s, v_s, s_s = mixer(
            xs, "s",
            lambda pa: _retention(pa, ret_g, state_ret, layer=l, batch=db, seq=t_new, chunk=t_new, nb=min(db, 8),
                                  out_dtype=F32, name=f"retention_s{l}"),
            lambda pa, kk, vv: _diff_attn_sample(pa, kk, vv, cache_k, cache_v, page_table, lam_p, diff_g, layer=l,
                                                 t_new=t_new, lam_init=lam_init, name=f"diff_attn_s{l}"))
        xs = _mem_attn_sample(xs, cache_mem_k, cache_mem_v, w_mem_q, w_mem_o, ln_g, ln_b, layer=l, t_new=t_new,
                              alpha=alpha, name=f"mem_attn_s{l}")
        xs = _ffn(xs, w_ffn_gu, w_ffn_down, ln_g, ln_b, layer=l, alpha=alpha, name=f"ffn_s{l}")

        for lst, val in zip(outs, (
                k_p.reshape(batch, seq, D_HEADS, 2, D_DK), v_p.reshape(batch, seq, D_HEADS, D_DV),
                k_s.reshape(db, t_new, D_HEADS, 2, D_DK), v_s.reshape(db, t_new, D_HEADS, D_DV),
                s_p, s_s,
                mk_p.reshape(batch, mem_len, M_HEADS, M_DH), mv_p.reshape(batch, mem_len, M_HEADS, M_DH))):
            lst.append(val)

    return (xp.reshape(batch, seq, D_MODEL), xs.reshape(db, t_new, D_MODEL), *(jnp.stack(o) for o in outs))
```

```python
import functools
import math

import jax
import jax.numpy as jnp
from jax import lax
from jax.experimental import pallas as pl
from jax.experimental.pallas import tpu as pltpu

F32 = jnp.float32
BF16 = jnp.bfloat16

D_MODEL = 1024
R_HEADS, R_DK, R_DV = 4, 128, 256
D_HEADS, D_DK = 8, 64
D_DV = 2 * D_DK
M_HEADS = 4
M_DH = D_MODEL // M_HEADS
D_FF = -(-8 * D_MODEL // (3 * 256)) * 256
LN_EPS = 1e-5
RMS_EPS = 1e-6
NEG_INF = -1e30

R_QK = R_HEADS * R_DK
R_V = R_HEADS * R_DV
D_QK = D_HEADS * 2 * D_DK
D_V = D_HEADS * D_DV
OFF_RQ = 0
OFF_RK = OFF_RQ + R_QK
OFF_RV = OFF_RK + R_QK
OFF_RG = OFF_RV + R_V
OFF_DQ = OFF_RG + R_V
OFF_DK = OFF_DQ + D_QK
OFF_DV = OFF_DK + D_QK
OFF_GR = OFF_DV + D_V
OFF_GD = OFF_GR + D_MODEL
IN_WIDTH = OFF_GD + D_MODEL

LOG_GAMMAS = tuple(math.log(1.0 - 2.0 ** (-5.0 - h)) for h in range(R_HEADS))
ALIBI_SLOPES = tuple(2.0 ** (-8.0 * (h + 1) / D_HEADS) for h in range(D_HEADS))

VMEM_LIMIT_BYTES = 56 * 1024 * 1024


def _params(n_grid_dims):
    return pltpu.CompilerParams(dimension_semantics=("arbitrary",) * n_grid_dims,
                                vmem_limit_bytes=VMEM_LIMIT_BYTES)


def _select_const(idx, values):
    out = jnp.float32(values[-1])
    for i in range(len(values) - 2, -1, -1):
        out = jnp.where(idx == i, jnp.float32(values[i]), out)
    return out


def _dot(a, b):
    return jnp.dot(a, b, preferred_element_type=F32)


def _dot_nt(a, b):
    return lax.dot_general(a, b, (((1,), (1,)), ((), ())), preferred_element_type=F32)


def _dot_tn(a, b):
    return lax.dot_general(a, b, (((0,), (0,)), ((), ())), preferred_element_type=F32)


def _layer_norm(y, g, b):
    mu = jnp.mean(y, axis=-1, keepdims=True)
    yc = y - mu
    var = jnp.mean(yc * yc, axis=-1, keepdims=True)
    return yc * lax.rsqrt(var + LN_EPS) * g + b


def _head_rms(o, g):
    return o * lax.rsqrt(jnp.mean(o * o, axis=-1, keepdims=True) + RMS_EPS) * g


def _diff_lambda(lam_ref, lam_init):
    lp = lam_ref[...]
    a = jnp.sum(lp[0:1, :] * lp[1:2, :], axis=-1, keepdims=True)
    b = jnp.sum(lp[2:3, :] * lp[3:4, :], axis=-1, keepdims=True)
    return jnp.exp(a) - jnp.exp(b) + lam_init


def _mm_body(x_ref, w_ref, o_ref, xb_ref):
    @pl.when(pl.program_id(1) == 0)
    def _():
        xb_ref[...] = x_ref[...].astype(BF16)

    o_ref[...] = _dot(xb_ref[...], w_ref[...])


def _matmul(x, w, layer, col0, ncols, name):
    m, k = x.shape
    tm = min(m, 1024)
    tn = min(ncols, 512)
    cb = col0 // tn
    return pl.pallas_call(
        _mm_body,
        grid=(m // tm, ncols // tn),
        in_specs=[pl.BlockSpec((tm, k), lambda i, j: (i, 0)),
                  pl.BlockSpec((None, k, tn), lambda i, j: (layer, 0, j + cb))],
        out_specs=pl.BlockSpec((tm, tn), lambda i, j: (i, j)),
        out_shape=jax.ShapeDtypeStruct((m, ncols), F32),
        scratch_shapes=[pltpu.VMEM((tm, k), BF16)],
        compiler_params=_params(2),
        name=name,
    )(x, w)


def _stacked_call(body, *, prev, layer, depth, per_layer_shape, stacked_spec, **kwargs):
    out_shape = kwargs.pop("out_shape")
    out_specs = kwargs.pop("out_specs")
    in_specs = kwargs.pop("in_specs")
    stacked = jax.ShapeDtypeStruct((depth,) + per_layer_shape, F32)
    aliases = {}
    extra = []
    if layer > 0:
        in_specs = list(in_specs) + [pl.BlockSpec(memory_space=pl.ANY)]
        aliases = {len(in_specs) - 1: 0}
        extra = [prev]

        def body_with_prev(*refs, _body=body, _n_in=len(in_specs)):
            return _body(*refs[:_n_in - 1], *refs[_n_in:])
        body = body_with_prev
    call = pl.pallas_call(body, in_specs=in_specs, out_specs=[stacked_spec] + list(out_specs),
                          out_shape=[stacked] + list(out_shape), input_output_aliases=aliases, **kwargs)
    return lambda *args: call(*args, *extra)


def _mm_kt_body(x_ref, wt_ref, o_ref, xb_ref):
    @pl.when(pl.program_id(2) == 0)
    def _():
        xb_ref[...] = x_ref[...].astype(BF16)

    o_ref[...] = _dot_nt(wt_ref[...], xb_ref[...])


def _proj_keys_t(x, wt, prev, *, layer, depth, batch, seq, name):
    ts = min(seq, 1024)
    tn = 512
    ns = seq // ts
    return _stacked_call(
        _mm_kt_body, prev=prev, layer=layer, depth=depth, per_layer_shape=(batch, D_QK, seq),
        stacked_spec=pl.BlockSpec((None, None, tn, ts), lambda b, si, j: (layer, b, j, si)),
        grid=(batch, ns, D_QK // tn),
        in_specs=[pl.BlockSpec((ts, D_MODEL), lambda b, si, j: (b * ns + si, 0)),
                  pl.BlockSpec((tn, D_MODEL), lambda b, si, j: (j, 0))],
        out_specs=[], out_shape=[],
        scratch_shapes=[pltpu.VMEM((ts, D_MODEL), BF16)],
        compiler_params=_params(3),
        name=name,
    )(x, wt)[0]


def _mm_v_body(x_ref, w_ref, flat_ref, o_ref):
    v = _dot(x_ref[...].astype(BF16), w_ref[...])
    o_ref[...] = v
    tm = v.shape[0]
    for h in range(D_HEADS):
        flat_ref[pl.ds(h, tm, stride=D_HEADS), :] = v[:, h * D_DV:(h + 1) * D_DV]


def _proj_values(x, w, prev, *, layer, depth, name):
    m = x.shape[0]
    tm = min(m, 512)
    return _stacked_call(
        _mm_v_body, prev=prev, layer=layer, depth=depth, per_layer_shape=(m * D_HEADS, D_DV),
        stacked_spec=pl.BlockSpec((None, tm * D_HEADS, D_DV), lambda i: (layer, i, 0)),
        grid=(m // tm,),
        in_specs=[pl.BlockSpec((tm, D_MODEL), lambda i: (i, 0)),
                  pl.BlockSpec((None, D_MODEL, D_V), lambda i: (layer, 0, OFF_DV // D_V))],
        out_specs=[pl.BlockSpec((tm, D_V), lambda i: (i, 0))],
        out_shape=[jax.ShapeDtypeStruct((m, D_V), F32)],
        compiler_params=_params(1),
        name=name,
    )(x, w)


def _ret_body(*refs, chunk, nb, has_s0):
    if has_s0:
        q_ref, k_ref, v_ref, rg_ref, g_ref, s0_ref, s_out_ref, o_ref, s_scr = refs
    else:
        q_ref, k_ref, v_ref, rg_ref, g_ref, s_out_ref, o_ref, s_scr = refs
    h = pl.program_id(1)
    c = pl.program_id(2)
    lg = _select_const(h, LOG_GAMMAS)

    @pl.when(c == 0)
    def _():
        if has_s0:
            s_scr[...] = s0_ref[:, 0]
        else:
            s_scr[...] = jnp.zeros_like(s_scr)

    ii = lax.broadcasted_iota(jnp.int32, (chunk, chunk), 0)
    jj = lax.broadcasted_iota(jnp.int32, (chunk, chunk), 1)
    causal = ii >= jj
    dist = jnp.where(causal, (ii - jj).astype(F32), 0.0)
    decay = jnp.where(causal, jnp.exp(lg * dist), 0.0)
    pos = lax.broadcasted_iota(jnp.int32, (chunk, 1), 0).astype(F32)
    q_decay = jnp.exp(lg * (pos + 1.0))
    k_decay = jnp.exp(lg * (chunk - 1.0 - pos))
    s_decay = jnp.exp(jnp.full((1, 1), chunk, F32) * lg)
    g = g_ref[...]

    for n in range(nb):
        rows = slice(n * chunk, (n + 1) * chunk)
        q = q_ref[rows, :]
        k = k_ref[rows, :] * (R_DK ** -0.5)
        vb = v_ref[rows, :].astype(BF16)
        s_prev = s_scr[n]
        inner = _dot_nt(q.astype(BF16), k.astype(BF16)) * decay
        o = _dot(inner.astype(BF16), vb) + _dot((q * q_decay).astype(BF16), s_prev.astype(BF16))
        s_scr[n] = s_decay * s_prev + _dot_tn((k * k_decay).astype(BF16), vb)
        rg = rg_ref[rows, :]
        o_ref[rows, :] = (_head_rms(o, g) * (rg * jax.nn.sigmoid(rg))).astype(o_ref.dtype)

    @pl.when(c == pl.num_programs(2) - 1)
    def _():
        s_out_ref[:, 0] = s_scr[...]


def _retention(pa, ret_g, s0, prev_states, *, layer, depth, batch, seq, chunk, nb, out_dtype, name):
    nc = seq // chunk
    rows = nb * chunk
    has_s0 = s0 is not None

    def row_idx(b, c):
        return b * nc + c

    in_specs = [
        pl.BlockSpec((rows, R_DK), lambda b, h, c: (row_idx(b, c), OFF_RQ // R_DK + h)),
        pl.BlockSpec((rows, R_DK), lambda b, h, c: (row_idx(b, c), OFF_RK // R_DK + h)),
        pl.BlockSpec((rows, R_DV), lambda b, h, c: (row_idx(b, c), OFF_RV // R_DV + h)),
        pl.BlockSpec((rows, R_DV), lambda b, h, c: (row_idx(b, c), OFF_RG // R_DV + h)),
        pl.BlockSpec((1, R_DV), lambda b, h, c: (0, 0)),
    ]
    args = [pa, pa, pa, pa, ret_g]
    state_spec = pl.BlockSpec((None, nb, 1, R_DK, R_DV), lambda b, h, c: (layer, b, h, 0, 0))
    if has_s0:
        in_specs.append(state_spec)
        args.append(s0)
    states, o = _stacked_call(
        functools.partial(_ret_body, chunk=chunk, nb=nb, has_s0=has_s0),
        prev=prev_states, layer=layer, depth=depth, per_layer_shape=(batch, R_HEADS, R_DK, R_DV),
        stacked_spec=state_spec,
        grid=(batch // nb, R_HEADS, nc),
        in_specs=in_specs,
        out_specs=[pl.BlockSpec((rows, R_DV), lambda b, h, c: (row_idx(b, c), h))],
        out_shape=[jax.ShapeDtypeStruct((batch * seq, R_V), out_dtype)],
        scratch_shapes=[pltpu.VMEM((nb, R_DK, R_DV), F32)],
        compiler_params=_params(3),
        name=name,
    )(*args)
    return o, states


ATTN_COLS = 256
ALIBI_RADIX = 256


def _dattn_p_body(q_ref, kt_ref, v_ref, lam_ref, g_ref, o_ref, kb_scr, vt_scr, qs_scr, sa_scr, sb_scr, pt_scr, m_scr,
                  l_scr, a_scr, acc_scr, *, tq, tk, seq, lam_init):
    h = pl.program_id(1)
    qi = pl.program_id(2)
    slope = _select_const(h, ALIBI_SLOPES)
    n_kchunks = seq // tk

    @pl.when(qi == 0)
    def _():
        lane = lax.broadcasted_iota(jnp.int32, (tk, 128), 1)
        for j in range(n_kchunks):
            rows = slice(j * tk, (j + 1) * tk)
            kpos = j * tk + lax.broadcasted_iota(jnp.int32, (tk, 128), 0)
            aug = jnp.where(lane == 0, kpos // ALIBI_RADIX, jnp.where(lane == 1, kpos % ALIBI_RADIX, 0))
            kb_scr[rows, 0:D_DV] = kt_ref[:, rows].T.astype(BF16)
            kb_scr[rows, D_DV:2 * D_DV] = aug.astype(F32).astype(BF16)
            vt_scr[j] = v_ref[rows, :].T.astype(BF16)

    q = q_ref[...] * (D_DK ** -0.5)
    lane = lax.broadcasted_iota(jnp.int32, q.shape, 1)
    qs_scr[0:tq, 0:D_DV] = jnp.where(lane < D_DK, q, 0.0).astype(BF16)
    qs_scr[tq:2 * tq, 0:D_DV] = jnp.where(lane >= D_DK, q, 0.0).astype(BF16)
    lane2 = lax.broadcasted_iota(jnp.int32, (2 * tq, 128), 1)
    qs_scr[:, D_DV:2 * D_DV] = jnp.where(lane2 == 0, slope * ALIBI_RADIX,
                                         jnp.where(lane2 == 1, slope, 0.0)).astype(BF16)
    m_scr[...] = jnp.full_like(m_scr, NEG_INF)
    l_scr[...] = jnp.zeros_like(l_scr)
    acc_scr[...] = jnp.zeros_like(acc_scr)

    def scores(ki, s_scr):
        k = kb_scr[pl.ds(pl.multiple_of(ki * tk, tk), tk), :]
        for c in range(2 * tq // ATTN_COLS):
            cols = slice(c * ATTN_COLS, (c + 1) * ATTN_COLS)
            s_scr[:, cols] = _dot_nt(k, qs_scr[cols, :])

    def softmax_pv(ki, s_scr, diag):
        for c in range(2 * tq // 128):
            cols = slice(c * 128, (c + 1) * 128)
            q0 = (c * 128) % tq
            if diag is not None and q0 + 127 < diag * tk:
                pt_scr[:, cols] = jnp.zeros((tk, 128), BF16)
                a_scr[:, cols] = jnp.ones((1, 128), F32)
                continue
            s = s_scr[:, cols]
            if diag is not None and q0 < diag * tk + tk - 1:
                kloc = diag * tk + lax.broadcasted_iota(jnp.int32, s.shape, 0)
                qloc = q0 + lax.broadcasted_iota(jnp.int32, s.shape, 1)
                s = jnp.where(kloc <= qloc, s, NEG_INF)
            m_prev = m_scr[:, cols]
            m_new = jnp.maximum(m_prev, jnp.max(s, axis=0, keepdims=True))
            alpha = jnp.exp(m_prev - m_new)
            p = jnp.exp(s - m_new)
            l_scr[:, cols] = alpha * l_scr[:, cols] + jnp.sum(p, axis=0, keepdims=True)
            m_scr[:, cols] = m_new
            a_scr[:, cols] = alpha
            pt_scr[:, cols] = p.astype(BF16)
        acc_scr[...] = acc_scr[...] * a_scr[...] + _dot(vt_scr[ki], pt_scr[...])

    n_diag = tq // tk
    assert n_diag == 2
    scores(0, sa_scr)

    def chunk_pair(j, carry):
        scores(2 * j + 1, sb_scr)
        softmax_pv(2 * j, sa_scr, None)
        scores(2 * j + 2, sa_scr)
        softmax_pv(2 * j + 1, sb_scr, None)
        return carry

    lax.fori_loop(0, qi, chunk_pair, 0)
    n_full = qi * n_diag
    scores(n_full + 1, sb_scr)
    softmax_pv(n_full, sa_scr, 0)
    softmax_pv(n_full + 1, sb_scr, 1)

    lam = _diff_lambda(lam_ref, lam_init)
    o = acc_scr[...] * (1.0 / l_scr[...])
    o = o[:, 0:tq] - lam * o[:, tq:2 * tq]
    g = jnp.concatenate([g_ref[...]] * (tq // 128), axis=1)
    o = o * lax.rsqrt(jnp.mean(o * o, axis=0, keepdims=True) + RMS_EPS) * g * (1.0 - lam_init)
    o_ref[...] = o.T.astype(o_ref.dtype)


def _diff_attn_prompt(pa, keys_t, vv, lam_p, diff_g, *, layer, batch, seq, lam_init, name):
    tq = min(seq, 512)
    tk = tq // 2
    nq = seq // tq
    assert all(s == 2.0 ** round(math.log2(s)) for s in ALIBI_SLOPES) and seq <= ALIBI_RADIX * ALIBI_RADIX
    g_tile = jnp.broadcast_to(diff_g.reshape(D_DV, 1), (D_DV, 128))
    return pl.pallas_call(
        functools.partial(_dattn_p_body, tq=tq, tk=tk, seq=seq, lam_init=lam_init),
        grid=(batch, D_HEADS, nq),
        in_specs=[
            pl.BlockSpec((tq, D_DV), lambda b, h, qi: (b * nq + qi, OFF_DQ // D_DV + h)),
            pl.BlockSpec((None, None, D_DV, seq), lambda b, h, qi: (layer, b, h, 0)),
            pl.BlockSpec((seq, D_DV), lambda b, h, qi: (b, h)),
            pl.BlockSpec((4, D_DK), lambda b, h, qi: (0, 0)),
            pl.BlockSpec((D_DV, 128), lambda b, h, qi: (0, 0)),
        ],
        out_specs=pl.BlockSpec((tq, D_DV), lambda b, h, qi: (b * nq + qi, h)),
        out_shape=jax.ShapeDtypeStruct((batch * seq, D_V), BF16),
        scratch_shapes=[
            pltpu.VMEM((seq, 2 * D_DV), BF16),
            pltpu.VMEM((seq // tk, D_DV, tk), BF16),
            pltpu.VMEM((2 * tq, 2 * D_DV), BF16),
            pltpu.VMEM((tk, 2 * tq), F32),
            pltpu.VMEM((tk, 2 * tq), F32),
            pltpu.VMEM((tk, 2 * tq), BF16),
            pltpu.VMEM((1, 2 * tq), F32),
            pltpu.VMEM((1, 2 * tq), F32),
            pltpu.VMEM((1, 2 * tq), F32),
            pltpu.VMEM((D_DV, 2 * tq), F32),
        ],
        compiler_params=_params(3),
        name=name,
    )(pa, keys_t, vv, lam_p, g_tile)


def _block_diag_rows(q, n_groups, group_width):
    t = q.shape[0]
    qt = jnp.concatenate([q] * n_groups, axis=0)
    rg = lax.broadcasted_iota(jnp.int32, qt.shape, 0) // t
    cg = lax.broadcasted_iota(jnp.int32, qt.shape, 1) // group_width
    return jnp.where(rg == cg, qt, 0.0)


def _dattn_s_body(pt_ref, q_ref, kn_ref, vn_ref, *rest, n_pages, page, t_new, lam_init):
    k_refs = rest[:n_pages]
    v_refs = rest[n_pages:2 * n_pages]
    lam_ref, g_ref, o_ref = rest[2 * n_pages:]
    past = n_pages * page
    n_rows = 2 * D_HEADS * t_new

    qbd = _block_diag_rows(q_ref[...] * (D_DK ** -0.5), 2 * D_HEADS, D_DK).astype(BF16)
    r = lax.broadcasted_iota(jnp.int32, (n_rows, 1), 0)
    slope = jnp.exp2(-8.0 * ((r // (2 * t_new)) + 1).astype(F32) / D_HEADS)
    qpos = past + (r % t_new)

    s_past = jnp.concatenate([_dot(qbd, k_refs[j][...].astype(BF16)) for j in range(n_pages)], axis=1)
    kpos = lax.broadcasted_iota(jnp.int32, (1, past), 1)
    s_past = s_past - slope * (qpos - kpos).astype(F32)

    pad = jnp.zeros((page - t_new, D_QK), F32)
    kn = jnp.concatenate([kn_ref[...], pad], axis=0).astype(BF16)
    vn = jnp.concatenate([vn_ref[...], pad], axis=0).astype(BF16)
    d_new = qpos - (past + lax.broadcasted_iota(jnp.int32, (1, page), 1))
    s_new = jnp.where(d_new >= 0, _dot_nt(qbd, kn) - slope * d_new.astype(F32), NEG_INF)

    m = jnp.maximum(jnp.max(s_past, axis=-1, keepdims=True), jnp.max(s_new, axis=-1, keepdims=True))
    p_past = jnp.exp(s_past - m)
    p_new = jnp.exp(s_new - m)
    denom = jnp.sum(p_past, axis=-1, keepdims=True) + jnp.sum(p_new, axis=-1, keepdims=True)
    acc = _dot(p_new.astype(BF16), vn)
    for j in range(n_pages):
        v_page = jnp.concatenate([v_refs[j][pl.ds(h, page, stride=D_HEADS), :] for h in range(D_HEADS)], axis=1)
        acc = acc + _dot(p_past[:, j * page:(j + 1) * page].astype(BF16), v_page.astype(BF16))
    o = acc / denom

    lam = _diff_lambda(lam_ref, lam_init)
    g = g_ref[...]
    heads = []
    for h in range(D_HEADS):
        cols = slice(h * D_DV, (h + 1) * D_DV)
        o1 = o[(2 * h) * t_new:(2 * h + 1) * t_new, cols]
        o2 = o[(2 * h + 1) * t_new:(2 * h + 2) * t_new, cols]
        heads.append(_head_rms(o1 - lam * o2, g) * (1.0 - lam_init))
    o_ref[...] = jnp.concatenate(heads, axis=1)


def _diff_attn_sample(pa, kk, vv, cache_k, cache_v, page_table, lam_p, diff_g, *, layer, t_new, lam_init, name):
    db, n_pages = page_table.shape
    page = cache_k.shape[3]
    pt = page_table.reshape(-1)

    def page_spec(j, shape):
        return pl.BlockSpec((None, None) + shape, lambda b, pt_ref: (layer, pt_ref[b * n_pages + j], 0, 0))

    grid_spec = pltpu.PrefetchScalarGridSpec(
        num_scalar_prefetch=1,
        grid=(db,),
        in_specs=[pl.BlockSpec((t_new, D_QK), lambda b, pt_ref: (b, OFF_DQ // D_QK)),
                  pl.BlockSpec((t_new, D_QK), lambda b, pt_ref: (b, 0)),
                  pl.BlockSpec((t_new, D_V), lambda b, pt_ref: (b, 0))]
        + [page_spec(j, (D_QK, page)) for j in range(n_pages)]
        + [page_spec(j, (page * D_HEADS, D_DV)) for j in range(n_pages)]
        + [pl.BlockSpec((4, D_DK), lambda b, pt_ref: (0, 0)),
           pl.BlockSpec((1, D_DV), lambda b, pt_ref: (0, 0))],
        out_specs=pl.BlockSpec((t_new, D_V), lambda b, pt_ref: (b, 0)),
    )
    return pl.pallas_call(
        functools.partial(_dattn_s_body, n_pages=n_pages, page=page, t_new=t_new, lam_init=lam_init),
        grid_spec=grid_spec,
        out_shape=jax.ShapeDtypeStruct((db * t_new, D_V), F32),
        compiler_params=_params(1),
        name=name,
    )(pt, pa, kk, vv, *([cache_k] * n_pages), *([cache_v] * n_pages), lam_p, diff_g)


def _mix_body(ro_ref, do_ref, gr_ref, gd_ref, x_ref, wr_ref, wd_ref, wo_ref, g_ref, b_ref, o_ref, *, alpha):
    ret_branch = _dot(ro_ref[...].astype(BF16), wr_ref[...])
    diff_branch = _dot(do_ref[...].astype(BF16), wd_ref[...])
    merged = jax.nn.sigmoid(gr_ref[...]) * ret_branch + jax.nn.sigmoid(gd_ref[...]) * diff_branch
    mix = _dot(merged.astype(BF16), wo_ref[...])
    o_ref[...] = _layer_norm(alpha * x_ref[...] + mix, g_ref[...], b_ref[...])


def _mixer_out(ro, do, gates, x, w_ret_o, w_diff_o, w_out, ln_g, ln_b, *, layer, alpha, name):
    m = x.shape[0]
    tm = min(m, 512)
    row = lambda i: (i, 0)
    wspec = pl.BlockSpec((None, D_MODEL, D_MODEL), lambda i: (layer, 0, 0))
    lnspec = pl.BlockSpec((None, None, 1, D_MODEL), lambda i: (layer, 0, 0, 0))
    return pl.pallas_call(
        functools.partial(_mix_body, alpha=alpha),
        grid=(m // tm,),
        in_specs=[pl.BlockSpec((tm, R_V), row), pl.BlockSpec((tm, D_V), row),
                  pl.BlockSpec((tm, D_MODEL), lambda i: (i, 0)), pl.BlockSpec((tm, D_MODEL), lambda i: (i, 1)),
                  pl.BlockSpec((tm, D_MODEL), row), wspec, wspec, wspec, lnspec, lnspec],
        out_specs=pl.BlockSpec((tm, D_MODEL), row),
        out_shape=jax.ShapeDtypeStruct((m, D_MODEL), F32),
        compiler_params=_params(1),
        name=name,
    )(ro, do, gates, gates, x, w_ret_o, w_diff_o, w_out, ln_g, ln_b)


def _softmax_rows(s):
    m = jnp.max(s, axis=-1, keepdims=True)
    e = jnp.exp(s - m)
    return e / jnp.sum(e, axis=-1, keepdims=True)


def _mem_p_body(x_ref, mk_ref, mv_ref, wq_ref, wo_ref, g_ref, b_ref, o_ref, *, alpha):
    x = x_ref[...]
    q = _dot(x.astype(BF16), wq_ref[...])
    heads = []
    for h in range(M_HEADS):
        cols = slice(h * M_DH, (h + 1) * M_DH)
        s = _dot_nt(q[:, cols].astype(BF16), mk_ref[:, cols].astype(BF16)) * (M_DH ** -0.5)
        heads.append(_dot(_softmax_rows(s).astype(BF16), mv_ref[:, cols].astype(BF16)))
    o = jnp.concatenate(heads, axis=1)
    att = _dot(o.astype(BF16), wo_ref[...])
    o_ref[...] = _layer_norm(alpha * x + att, g_ref[...], b_ref[...])


def _mem_attn_prompt(x, mk, mv, w_q, w_o, ln_g, ln_b, *, layer, seq, alpha, name):
    m = x.shape[0]
    mem_len = mk.shape[0] // (m // seq)
    tm = min(seq, 512)
    per_batch = seq // tm
    row = lambda i: (i, 0)
    wspec = pl.BlockSpec((None, D_MODEL, D_MODEL), lambda i: (layer, 0, 0))
    lnspec = pl.BlockSpec((None, None, 1, D_MODEL), lambda i: (layer, 1, 0, 0))
    mspec = pl.BlockSpec((mem_len, D_MODEL), lambda i: (i // per_batch, 0))
    return pl.pallas_call(
        functools.partial(_mem_p_body, alpha=alpha),
        grid=(m // tm,),
        in_specs=[pl.BlockSpec((tm, D_MODEL), row), mspec, mspec, wspec, wspec, lnspec, lnspec],
        out_specs=pl.BlockSpec((tm, D_MODEL), row),
        out_shape=jax.ShapeDtypeStruct((m, D_MODEL), F32),
        compiler_params=_params(1),
        name=name,
    )(x, mk, mv, w_q, w_o, ln_g, ln_b)


def _mem_s_body(x_ref, mk_ref, mv_ref, wq_ref, wo_ref, g_ref, b_ref, o_ref, *, alpha, n_samples, t_new):
    x = x_ref[...]
    q = _dot(x.astype(BF16), wq_ref[...])
    outs = []
    mem_len = mk_ref.shape[1] // (D_MODEL // 128)
    halves = M_DH // 128

    def mem_rows(ref, n):
        return jnp.concatenate(
            [ref[n, pl.ds(j * M_HEADS + h, mem_len, stride=M_HEADS * halves), :]
             for h in range(M_HEADS) for j in range(halves)], axis=1).astype(BF16)

    for n in range(n_samples):
        qbd = _block_diag_rows(q[n * t_new:(n + 1) * t_new, :], M_HEADS, M_DH).astype(BF16)
        s = _dot_nt(qbd, mem_rows(mk_ref, n)) * (M_DH ** -0.5)
        o = _dot(_softmax_rows(s).astype(BF16), mem_rows(mv_ref, n))
        outs.append(jnp.concatenate(
            [o[h * t_new:(h + 1) * t_new, h * M_DH:(h + 1) * M_DH] for h in range(M_HEADS)], axis=1))
    o = jnp.concatenate(outs, axis=0)
    att = _dot(o.astype(BF16), wo_ref[...])
    o_ref[...] = _layer_norm(alpha * x + att, g_ref[...], b_ref[...])


def _mem_attn_sample(x, cache_mk, cache_mv, w_q, w_o, ln_g, ln_b, *, layer, t_new, alpha, name):
    m = x.shape[0]
    db, mem_rows = cache_mk.shape[1:3]
    ns = min(db, 4)
    rows = ns * t_new
    row = lambda i: (i, 0)
    wspec = pl.BlockSpec((None, D_MODEL, D_MODEL), lambda i: (layer, 0, 0))
    lnspec = pl.BlockSpec((None, None, 1, D_MODEL), lambda i: (layer, 1, 0, 0))
    mspec = pl.BlockSpec((None, ns, mem_rows, 128), lambda i: (layer, i, 0, 0))
    return pl.pallas_call(
        functools.partial(_mem_s_body, alpha=alpha, n_samples=ns, t_new=t_new),
        grid=(db // ns,),
        in_specs=[pl.BlockSpec((rows, D_MODEL), row), mspec, mspec, wspec, wspec, lnspec, lnspec],
        out_specs=pl.BlockSpec((rows, D_MODEL), row),
        out_shape=jax.ShapeDtypeStruct((m, D_MODEL), F32),
        compiler_params=_params(1),
        name=name,
    )(x, cache_mk, cache_mv, w_q, w_o, ln_g, ln_b)


FF_CHUNK = 256


def _ffn_body(x_ref, wgu_ref, wd_ref, g_ref, b_ref, o_ref, *, alpha):
    x = x_ref[...]
    xb = x.astype(BF16)
    y = alpha * x
    for c in range(D_FF // FF_CHUNK):
        gate = _dot(xb, wgu_ref[:, c * FF_CHUNK:(c + 1) * FF_CHUNK])
        up = _dot(xb, wgu_ref[:, D_FF + c * FF_CHUNK:D_FF + (c + 1) * FF_CHUNK])
        hidden = (gate * jax.nn.sigmoid(gate) * up).astype(BF16)
        y = y + _dot(hidden, wd_ref[c * FF_CHUNK:(c + 1) * FF_CHUNK, :])
    o_ref[...] = _layer_norm(y, g_ref[...], b_ref[...])


def _ffn(x, w_gu, w_down, ln_g, ln_b, *, layer, alpha, name):
    m = x.shape[0]
    tm = min(m, 512)
    row = lambda i: (i, 0)
    lnspec = pl.BlockSpec((None, None, 1, D_MODEL), lambda i: (layer, 2, 0, 0))
    return pl.pallas_call(
        functools.partial(_ffn_body, alpha=alpha),
        grid=(m // tm,),
        in_specs=[pl.BlockSpec((tm, D_MODEL), row),
                  pl.BlockSpec((None, D_MODEL, 2 * D_FF), lambda i: (layer, 0, 0)),
                  pl.BlockSpec((None, D_FF, D_MODEL), lambda i: (layer, 0, 0)),
                  lnspec, lnspec],
        out_specs=pl.BlockSpec((tm, D_MODEL), row),
        out_shape=jax.ShapeDtypeStruct((m, D_MODEL), F32),
        compiler_params=_params(1),
        name=name,
    )(x, w_gu, w_down, ln_g, ln_b)


def kernel(x_prompt, x_sample, mem_prompt, cache_k, cache_v, state_ret, cache_mem_k, cache_mem_v, page_table, w_in, w_ret_o, w_diff_o, w_out, ret_norm_g, diff_norm_g, diff_lambda, w_mem_q, w_mem_kv, w_mem_o, w_ffn_gu, w_ffn_down, ln_g, ln_b):
    depth = w_in.shape[0]
    batch, seq, _ = x_prompt.shape
    db, t_new, _ = x_sample.shape
    mem_len = mem_prompt.shape[1]
    n_pool, page = cache_k.shape[1:3]
    alpha = (2 * depth) ** 0.25

    w_in, w_ret_o, w_diff_o, w_out, w_mem_q, w_mem_kv, w_mem_o, w_ffn_gu, w_ffn_down = (
        w.astype(BF16) for w in (w_in, w_ret_o, w_diff_o, w_out, w_mem_q, w_mem_kv, w_mem_o, w_ffn_gu, w_ffn_down))
    cache_k = jnp.transpose(cache_k, (0, 1, 3, 4, 5, 2)).reshape(depth, n_pool, D_QK, page)
    cache_v = cache_v.reshape(depth, n_pool, page * D_HEADS, D_DV)

    def mem_view(c):
        c = c.reshape(depth, db, mem_len, M_HEADS, M_DH // 128, 128)
        return jnp.transpose(c, (0, 1, 2, 4, 3, 5)).reshape(depth, db, mem_len * (D_MODEL // 128), 128)

    cache_mem_k = mem_view(cache_mem_k)
    cache_mem_v = mem_view(cache_mem_v)
    ln_g = ln_g.reshape(depth, 3, 1, D_MODEL)
    ln_b = ln_b.reshape(depth, 3, 1, D_MODEL)
    mem = mem_prompt.reshape(batch * mem_len, D_MODEL)
    xp = x_prompt.reshape(batch * seq, D_MODEL)
    xs = x_sample.reshape(db * t_new, D_MODEL)

    keys_t = values_flat = states_p = states_s = None
    outs = [[] for _ in range(4)]
    for l in range(depth):
        lam_init = 0.8 - 0.6 * math.exp(-0.3 * l)
        ret_g = ret_norm_g[l].reshape(1, R_DV)
        diff_g = diff_norm_g[l].reshape(1, D_DV)
        lam_p = diff_lambda[l]
        wk_t = w_in[l, :, OFF_DK:OFF_DV].T

        mk_p = _matmul(mem, w_mem_kv, l, 0, D_MODEL, f"mem_k_{l}")
        mv_p = _matmul(mem, w_mem_kv, l, D_MODEL, D_MODEL, f"mem_v_{l}")

        pa = _matmul(xp, w_in, l, 0, OFF_DK, f"proj_a_p{l}")
        keys_t = _proj_keys_t(xp, wk_t, keys_t, layer=l, depth=depth, batch=batch, seq=seq, name=f"proj_k_p{l}")
        values_flat, vv = _proj_values(xp, w_in, values_flat, layer=l, depth=depth, name=f"proj_v_p{l}")
        gates = _matmul(xp, w_in, l, OFF_GR, 2 * D_MODEL, f"proj_g_p{l}")
        ro, states_p = _retention(pa, ret_g, None, states_p, layer=l, depth=depth, batch=batch, seq=seq,
                                  chunk=min(seq, 256), nb=1, out_dtype=BF16, name=f"retention_p{l}")
        do = _diff_attn_prompt(pa, keys_t, vv, lam_p, diff_g, layer=l, batch=batch, seq=seq, lam_init=lam_init,
                               name=f"diff_attn_p{l}")
        xp = _mixer_out(ro, do, gates, xp, w_ret_o, w_diff_o, w_out, ln_g, ln_b, layer=l, alpha=alpha,
                        name=f"mixer_out_p{l}")
        xp = _mem_attn_prompt(xp, mk_p, mv_p, w_mem_q, w_mem_o, ln_g, ln_b, layer=l, seq=seq, alpha=alpha,
                              name=f"mem_attn_p{l}")
        xp = _ffn(xp, w_ffn_gu, w_ffn_down, ln_g, ln_b, layer=l, alpha=alpha, name=f"ffn_p{l}")

        pa = _matmul(xs, w_in, l, 0, OFF_DK, f"proj_a_s{l}")
        k_s = _matmul(xs, w_in, l, OFF_DK, D_QK, f"proj_k_s{l}")
        v_s = _matmul(xs, w_in, l, OFF_DV, D_V, f"proj_v_s{l}")
        gates = _matmul(xs, w_in, l, OFF_GR, 2 * D_MODEL, f"proj_g_s{l}")
        ro, states_s = _retention(pa, ret_g, state_ret, states_s, layer=l, depth=depth, batch=db, seq=t_new,
                                  chunk=t_new, nb=min(db, 8), out_dtype=F32, name=f"retention_s{l}")
        do = _diff_attn_sample(pa, k_s, v_s, cache_k, cache_v, page_table, lam_p, diff_g, layer=l, t_new=t_new,
                               lam_init=lam_init, name=f"diff_attn_s{l}")
        xs = _mixer_out(ro, do, gates, xs, w_ret_o, w_diff_o, w_out, ln_g, ln_b, layer=l, alpha=alpha,
                        name=f"mixer_out_s{l}")
        xs = _mem_attn_sample(xs, cache_mem_k, cache_mem_v, w_mem_q, w_mem_o, ln_g, ln_b, layer=l, t_new=t_new,
                              alpha=alpha, name=f"mem_attn_s{l}")
        xs = _ffn(xs, w_ffn_gu, w_ffn_down, ln_g, ln_b, layer=l, alpha=alpha, name=f"ffn_s{l}")

        for lst, val in zip(outs, (
                k_s.reshape(db, t_new, D_HEADS, 2, D_DK), v_s.reshape(db, t_new, D_HEADS, D_DV),
                mk_p.reshape(batch, mem_len, M_HEADS, M_DH), mv_p.reshape(batch, mem_len, M_HEADS, M_DH))):
            lst.append(val)

    k_prompt = jnp.transpose(keys_t.reshape(depth, batch, D_HEADS, 2, D_DK, seq), (0, 1, 5, 2, 3, 4))
    v_prompt = values_flat.reshape(depth, batch, seq, D_HEADS, D_DV)
    k_sample, v_sample, mem_k, mem_v = (jnp.stack(o) for o in outs)
    return (xp.reshape(batch, seq, D_MODEL), xs.reshape(db, t_new, D_MODEL), k_prompt, v_prompt, k_sample, v_sample,
            states_p, states_s, mem_k, mem_v)
```

```python
import functools
import math

import jax
import jax.numpy as jnp
from jax import lax
from jax.experimental import pallas as pl
from jax.experimental.pallas import tpu as pltpu

F32 = jnp.float32
BF16 = jnp.bfloat16

D_MODEL = 1024
R_HEADS, R_DK, R_DV = 4, 128, 256
D_HEADS, D_DK = 8, 64
D_DV = 2 * D_DK
M_HEADS = 4
M_DH = D_MODEL // M_HEADS
D_FF = -(-8 * D_MODEL // (3 * 256)) * 256
LN_EPS = 1e-5
RMS_EPS = 1e-6
NEG_INF = -1e30

R_QK = R_HEADS * R_DK
R_V = R_HEADS * R_DV
D_QK = D_HEADS * 2 * D_DK
D_V = D_HEADS * D_DV
OFF_RQ = 0
OFF_RK = OFF_RQ + R_QK
OFF_RV = OFF_RK + R_QK
OFF_RG = OFF_RV + R_V
OFF_DQ = OFF_RG + R_V
OFF_DK = OFF_DQ + D_QK
OFF_DV = OFF_DK + D_QK
OFF_GR = OFF_DV + D_V
OFF_GD = OFF_GR + D_MODEL
IN_WIDTH = OFF_GD + D_MODEL

LOG_GAMMAS = tuple(math.log(1.0 - 2.0 ** (-5.0 - h)) for h in range(R_HEADS))
ALIBI_SLOPES = tuple(2.0 ** (-8.0 * (h + 1) / D_HEADS) for h in range(D_HEADS))

VMEM_LIMIT_BYTES = 56 * 1024 * 1024


def _params(n_grid_dims):
    return pltpu.CompilerParams(dimension_semantics=("arbitrary",) * n_grid_dims,
                                vmem_limit_bytes=VMEM_LIMIT_BYTES)


def _select_const(idx, values):
    out = jnp.float32(values[-1])
    for i in range(len(values) - 2, -1, -1):
        out = jnp.where(idx == i, jnp.float32(values[i]), out)
    return out


def _dot(a, b):
    return jnp.dot(a, b, preferred_element_type=F32)


def _dot_nt(a, b):
    return lax.dot_general(a, b, (((1,), (1,)), ((), ())), preferred_element_type=F32)


def _dot_tn(a, b):
    return lax.dot_general(a, b, (((0,), (0,)), ((), ())), preferred_element_type=F32)


def _layer_norm(y, g, b):
    mu = jnp.mean(y, axis=-1, keepdims=True)
    yc = y - mu
    var = jnp.mean(yc * yc, axis=-1, keepdims=True)
    return yc * lax.rsqrt(var + LN_EPS) * g + b


def _head_rms(o, g):
    return o * lax.rsqrt(jnp.mean(o * o, axis=-1, keepdims=True) + RMS_EPS) * g


def _diff_lambda(lam_ref, lam_init):
    lp = lam_ref[...]
    a = jnp.sum(lp[0:1, :] * lp[1:2, :], axis=-1, keepdims=True)
    b = jnp.sum(lp[2:3, :] * lp[3:4, :], axis=-1, keepdims=True)
    return jnp.exp(a) - jnp.exp(b) + lam_init


def _mm_body(x_ref, w_ref, o_ref, xb_ref):
    @pl.when(pl.program_id(1) == 0)
    def _():
        xb_ref[...] = x_ref[...].astype(BF16)

    o_ref[...] = _dot(xb_ref[...], w_ref[...])


def _matmul(x, w, layer, col0, ncols, name):
    m, k = x.shape
    tm = min(m, 2048)
    tn = min(ncols, 512)
    cb = col0 // tn
    return pl.pallas_call(
        _mm_body,
        grid=(m // tm, ncols // tn),
        in_specs=[pl.BlockSpec((tm, k), lambda i, j: (i, 0)),
                  pl.BlockSpec((None, k, tn), lambda i, j: (layer, 0, j + cb))],
        out_specs=pl.BlockSpec((tm, tn), lambda i, j: (i, j)),
        out_shape=jax.ShapeDtypeStruct((m, ncols), F32),
        scratch_shapes=[pltpu.VMEM((tm, k), BF16)],
        compiler_params=_params(2),
        name=name,
    )(x, w)


def _stacked_call(body, *, prev, layer, depth, per_layer_shape, stacked_spec, **kwargs):
    out_shape = kwargs.pop("out_shape")
    out_specs = kwargs.pop("out_specs")
    in_specs = kwargs.pop("in_specs")
    stacked = jax.ShapeDtypeStruct((depth,) + per_layer_shape, F32)
    aliases = {}
    extra = []
    if layer > 0:
        in_specs = list(in_specs) + [pl.BlockSpec(memory_space=pl.ANY)]
        aliases = {len(in_specs) - 1: 0}
        extra = [prev]

        def body_with_prev(*refs, _body=body, _n_in=len(in_specs)):
            return _body(*refs[:_n_in - 1], *refs[_n_in:])
        body = body_with_prev
    call = pl.pallas_call(body, in_specs=in_specs, out_specs=[stacked_spec] + list(out_specs),
                          out_shape=[stacked] + list(out_shape), input_output_aliases=aliases, **kwargs)
    return lambda *args: call(*args, *extra)


def _mm_kt_body(x_ref, wt_ref, o_ref, xb_ref):
    @pl.when(pl.program_id(2) == 0)
    def _():
        xb_ref[...] = x_ref[...].astype(BF16)

    o_ref[...] = _dot_nt(wt_ref[...], xb_ref[...])


def _proj_keys_t(x, wt, prev, *, layer, depth, batch, seq, name):
    ts = min(seq, 1024)
    tn = 512
    ns = seq // ts
    return _stacked_call(
        _mm_kt_body, prev=prev, layer=layer, depth=depth, per_layer_shape=(batch, D_QK, seq),
        stacked_spec=pl.BlockSpec((None, None, tn, ts), lambda b, si, j: (layer, b, j, si)),
        grid=(batch, ns, D_QK // tn),
        in_specs=[pl.BlockSpec((ts, D_MODEL), lambda b, si, j: (b * ns + si, 0)),
                  pl.BlockSpec((tn, D_MODEL), lambda b, si, j: (j, 0))],
        out_specs=[], out_shape=[],
        scratch_shapes=[pltpu.VMEM((ts, D_MODEL), BF16)],
        compiler_params=_params(3),
        name=name,
    )(x, wt)[0]


def _mm_v_body(x_ref, w_ref, flat_ref, o_ref):
    v = _dot(x_ref[...].astype(BF16), w_ref[...])
    o_ref[...] = v
    tm = v.shape[0]
    for h in range(D_HEADS):
        flat_ref[pl.ds(h, tm, stride=D_HEADS), :] = v[:, h * D_DV:(h + 1) * D_DV]


def _proj_values(x, w, prev, *, layer, depth, name):
    m = x.shape[0]
    tm = min(m, 512)
    return _stacked_call(
        _mm_v_body, prev=prev, layer=layer, depth=depth, per_layer_shape=(m * D_HEADS, D_DV),
        stacked_spec=pl.BlockSpec((None, tm * D_HEADS, D_DV), lambda i: (layer, i, 0)),
        grid=(m // tm,),
        in_specs=[pl.BlockSpec((tm, D_MODEL), lambda i: (i, 0)),
                  pl.BlockSpec((None, D_MODEL, D_V), lambda i: (layer, 0, OFF_DV // D_V))],
        out_specs=[pl.BlockSpec((tm, D_V), lambda i: (i, 0))],
        out_shape=[jax.ShapeDtypeStruct((m, D_V), F32)],
        compiler_params=_params(1),
        name=name,
    )(x, w)


def _ret_body(*refs, chunk, nb, has_s0):
    if has_s0:
        q_ref, k_ref, v_ref, rg_ref, g_ref, s0_ref, s_out_ref, o_ref, s_scr = refs
    else:
        q_ref, k_ref, v_ref, rg_ref, g_ref, s_out_ref, o_ref, s_scr = refs
    c = pl.program_id(1)

    @pl.when(c == 0)
    def _():
        if has_s0:
            s_scr[...] = s0_ref[...]
        else:
            s_scr[...] = jnp.zeros_like(s_scr)

    ii = lax.broadcasted_iota(jnp.int32, (chunk, chunk), 0)
    jj = lax.broadcasted_iota(jnp.int32, (chunk, chunk), 1)
    causal = ii >= jj
    dist = jnp.where(causal, (ii - jj).astype(F32), 0.0)
    pos = lax.broadcasted_iota(jnp.int32, (chunk, 1), 0).astype(F32)
    g = g_ref[...]

    for h in range(R_HEADS):
        lg = LOG_GAMMAS[h]
        decay = jnp.where(causal, jnp.exp(lg * dist), 0.0)
        q_decay = jnp.exp(lg * (pos + 1.0))
        k_decay = jnp.exp(lg * (chunk - 1.0 - pos))
        s_decay = math.exp(lg * chunk)
        qk_cols = slice(h * R_DK, (h + 1) * R_DK)
        v_cols = slice(h * R_DV, (h + 1) * R_DV)
        for n in range(nb):
            rows = slice(n * chunk, (n + 1) * chunk)
            q = q_ref[rows, qk_cols]
            k = k_ref[rows, qk_cols] * (R_DK ** -0.5)
            vb = v_ref[rows, v_cols].astype(BF16)
            s_prev = s_scr[n, h]
            inner = _dot_nt(q.astype(BF16), k.astype(BF16)) * decay
            o = _dot(inner.astype(BF16), vb) + _dot((q * q_decay).astype(BF16), s_prev.astype(BF16))
            s_scr[n, h] = s_decay * s_prev + _dot_tn((k * k_decay).astype(BF16), vb)
            rg = rg_ref[rows, v_cols]
            o_ref[rows, v_cols] = (_head_rms(o, g) * (rg * jax.nn.sigmoid(rg))).astype(o_ref.dtype)

    @pl.when(c == pl.num_programs(1) - 1)
    def _():
        s_out_ref[...] = s_scr[...]


def _retention(pa, ret_g, s0, prev_states, *, layer, depth, batch, seq, chunk, nb, out_dtype, name):
    nc = seq // chunk
    rows = nb * chunk
    has_s0 = s0 is not None

    def row_idx(b, c):
        return b * nc + c

    in_specs = [
        pl.BlockSpec((rows, R_QK), lambda b, c: (row_idx(b, c), OFF_RQ // R_QK)),
        pl.BlockSpec((rows, R_QK), lambda b, c: (row_idx(b, c), OFF_RK // R_QK)),
        pl.BlockSpec((rows, R_V), lambda b, c: (row_idx(b, c), OFF_RV // R_V)),
        pl.BlockSpec((rows, R_V), lambda b, c: (row_idx(b, c), OFF_RG // R_V)),
        pl.BlockSpec((1, R_DV), lambda b, c: (0, 0)),
    ]
    args = [pa, pa, pa, pa, ret_g]
    state_spec = pl.BlockSpec((None, nb, R_HEADS, R_DK, R_DV), lambda b, c: (layer, b, 0, 0, 0))
    if has_s0:
        in_specs.append(state_spec)
        args.append(s0)
    states, o = _stacked_call(
        functools.partial(_ret_body, chunk=chunk, nb=nb, has_s0=has_s0),
        prev=prev_states, layer=layer, depth=depth, per_layer_shape=(batch, R_HEADS, R_DK, R_DV),
        stacked_spec=state_spec,
        grid=(batch // nb, nc),
        in_specs=in_specs,
        out_specs=[pl.BlockSpec((rows, R_V), lambda b, c: (row_idx(b, c), 0))],
        out_shape=[jax.ShapeDtypeStruct((batch * seq, R_V), out_dtype)],
        scratch_shapes=[pltpu.VMEM((nb, R_HEADS, R_DK, R_DV), F32)],
        compiler_params=_params(2),
        name=name,
    )(*args)
    return o, states


ATTN_COLS = 256
ALIBI_RADIX = 256


LOG2_E = math.log2(math.e)
ACC_PAD = 16


def _dattn_p_body(q_ref, kt_ref, v_ref, lam_ref, g_ref, o_ref, kb_scr, vt_scr, qs_scr, sa_scr, sb_scr, pt_scr, m_scr,
                  a_scr, acc_scr, *, tq, tk, seq, lam_init):
    h = pl.program_id(1)
    qi = pl.program_id(2)
    slope = _select_const(h, ALIBI_SLOPES)
    n_kchunks = seq // tk

    @pl.when(qi == 0)
    def _():
        lane = lax.broadcasted_iota(jnp.int32, (tk, 128), 1)
        ones_row = jnp.where(lax.broadcasted_iota(jnp.int32, (ACC_PAD, tk), 0) == 0, 1.0, 0.0).astype(BF16)
        for j in range(n_kchunks):
            rows = slice(j * tk, (j + 1) * tk)
            kpos = j * tk + lax.broadcasted_iota(jnp.int32, (tk, 128), 0)
            aug = jnp.where(lane < 3, kpos // ALIBI_RADIX, jnp.where(lane < 6, kpos % ALIBI_RADIX, 0))
            kb_scr[rows, 0:D_DV] = kt_ref[:, rows].T.astype(BF16)
            kb_scr[rows, D_DV:2 * D_DV] = aug.astype(F32).astype(BF16)
            vt_scr[j, 0:D_DV, :] = v_ref[rows, :].T.astype(BF16)
            vt_scr[j, D_DV:D_DV + ACC_PAD, :] = ones_row

    q = q_ref[...] * (D_DK ** -0.5 * LOG2_E)
    lane = lax.broadcasted_iota(jnp.int32, q.shape, 1)
    qs_scr[0:tq, 0:D_DV] = jnp.where(lane < D_DK, q, 0.0).astype(BF16)
    qs_scr[tq:2 * tq, 0:D_DV] = jnp.where(lane >= D_DK, q, 0.0).astype(BF16)
    c = jnp.full((2 * tq, 128), slope * LOG2_E, F32)
    c1 = c.astype(BF16).astype(F32)
    c2 = (c - c1).astype(BF16).astype(F32)
    c3 = (c - c1 - c2).astype(BF16).astype(F32)
    lane2 = lax.broadcasted_iota(jnp.int32, (2 * tq, 128), 1)
    piece = jnp.where((lane2 == 0) | (lane2 == 3), c1, jnp.where((lane2 == 1) | (lane2 == 4), c2, c3))
    qs_scr[:, D_DV:2 * D_DV] = jnp.where(lane2 < 3, piece * ALIBI_RADIX,
                                         jnp.where(lane2 < 6, piece, 0.0)).astype(BF16)
    m_scr[...] = jnp.full_like(m_scr, NEG_INF)
    acc_scr[...] = jnp.zeros_like(acc_scr)

    def scores(ki, s_scr):
        k = kb_scr[pl.ds(pl.multiple_of(ki * tk, tk), tk), :]
        for c in range(2 * tq // ATTN_COLS):
            cols = slice(c * ATTN_COLS, (c + 1) * ATTN_COLS)
            s_scr[:, cols] = _dot_nt(k, qs_scr[cols, :])

    def softmax_pv(ki, s_scr, diag):
        for c in range(2 * tq // 128):
            cols = slice(c * 128, (c + 1) * 128)
            q0 = (c * 128) % tq
            if diag is not None and q0 + 127 < diag * tk:
                pt_scr[:, cols] = jnp.zeros((tk, 128), BF16)
                a_scr[:, cols] = jnp.ones((1, 128), F32)
                continue
            s = s_scr[:, cols]
            if diag is not None and q0 < diag * tk + tk - 1:
                kloc = diag * tk + lax.broadcasted_iota(jnp.int32, s.shape, 0)
                qloc = q0 + lax.broadcasted_iota(jnp.int32, s.shape, 1)
                s = jnp.where(kloc <= qloc, s, NEG_INF)
            m_prev = m_scr[:, cols]
            m_new = jnp.maximum(m_prev, jnp.max(s, axis=0, keepdims=True))
            m_scr[:, cols] = m_new
            a_scr[:, cols] = jnp.exp2(m_prev - m_new)
            pt_scr[:, cols] = jnp.exp2(s - m_new).astype(BF16)
        acc_scr[...] = acc_scr[...] * a_scr[...] + _dot(vt_scr[ki], pt_scr[...])

    n_diag = tq // tk
    assert n_diag == 2
    scores(0, sa_scr)

    def chunk_pair(j, carry):
        scores(2 * j + 1, sb_scr)
        softmax_pv(2 * j, sa_scr, None)
        scores(2 * j + 2, sa_scr)
        softmax_pv(2 * j + 1, sb_scr, None)
        return carry

    lax.fori_loop(0, qi, chunk_pair, 0)
    n_full = qi * n_diag
    scores(n_full + 1, sb_scr)
    softmax_pv(n_full, sa_scr, 0)
    softmax_pv(n_full + 1, sb_scr, 1)

    lam = _diff_lambda(lam_ref, lam_init)
    o = acc_scr[0:D_DV, :] * (1.0 / acc_scr[D_DV:D_DV + 1, :])
    o = o[:, 0:tq] - lam * o[:, tq:2 * tq]
    g = jnp.concatenate([g_ref[...]] * (tq // 128), axis=1)
    o = o * lax.rsqrt(jnp.mean(o * o, axis=0, keepdims=True) + RMS_EPS) * g * (1.0 - lam_init)
    o_ref[...] = o.T.astype(o_ref.dtype)


def _diff_attn_prompt(pa, keys_t, vv, lam_p, diff_g, *, layer, batch, seq, lam_init, name):
    tq = min(seq, 512)
    tk = tq // 2
    nq = seq // tq
    assert all(s == 2.0 ** round(math.log2(s)) for s in ALIBI_SLOPES) and seq <= ALIBI_RADIX * ALIBI_RADIX
    g_tile = jnp.broadcast_to(diff_g.reshape(D_DV, 1), (D_DV, 128))
    return pl.pallas_call(
        functools.partial(_dattn_p_body, tq=tq, tk=tk, seq=seq, lam_init=lam_init),
        grid=(batch, D_HEADS, nq),
        in_specs=[
            pl.BlockSpec((tq, D_DV), lambda b, h, qi: (b * nq + qi, OFF_DQ // D_DV + h)),
            pl.BlockSpec((None, None, D_DV, seq), lambda b, h, qi: (layer, b, h, 0)),
            pl.BlockSpec((seq, D_DV), lambda b, h, qi: (b, h)),
            pl.BlockSpec((4, D_DK), lambda b, h, qi: (0, 0)),
            pl.BlockSpec((D_DV, 128), lambda b, h, qi: (0, 0)),
        ],
        out_specs=pl.BlockSpec((tq, D_DV), lambda b, h, qi: (b * nq + qi, h)),
        out_shape=jax.ShapeDtypeStruct((batch * seq, D_V), BF16),
        scratch_shapes=[
            pltpu.VMEM((seq, 2 * D_DV), BF16),
            pltpu.VMEM((seq // tk, D_DV + ACC_PAD, tk), BF16),
            pltpu.VMEM((2 * tq, 2 * D_DV), BF16),
            pltpu.VMEM((tk, 2 * tq), F32),
            pltpu.VMEM((tk, 2 * tq), F32),
            pltpu.VMEM((tk, 2 * tq), BF16),
            pltpu.VMEM((1, 2 * tq), F32),
            pltpu.VMEM((1, 2 * tq), F32),
            pltpu.VMEM((D_DV + ACC_PAD, 2 * tq), F32),
        ],
        compiler_params=_params(3),
        name=name,
    )(pa, keys_t, vv, lam_p, g_tile)


def _block_diag_rows(q, n_groups, group_width):
    t = q.shape[0]
    qt = jnp.concatenate([q] * n_groups, axis=0)
    rg = lax.broadcasted_iota(jnp.int32, qt.shape, 0) // t
    cg = lax.broadcasted_iota(jnp.int32, qt.shape, 1) // group_width
    return jnp.where(rg == cg, qt, 0.0)


def _dattn_s_body(pt_ref, q_ref, kn_ref, vn_ref, *rest, n_pages, page, t_new, lam_init):
    k_refs = rest[:n_pages]
    v_refs = rest[n_pages:2 * n_pages]
    lam_ref, g_ref, o_ref = rest[2 * n_pages:]
    past = n_pages * page
    n_rows = 2 * D_HEADS * t_new

    qbd = _block_diag_rows(q_ref[...] * (D_DK ** -0.5), 2 * D_HEADS, D_DK).astype(BF16)
    r = lax.broadcasted_iota(jnp.int32, (n_rows, 1), 0)
    slope = jnp.exp2(-8.0 * ((r // (2 * t_new)) + 1).astype(F32) / D_HEADS)
    qpos = past + (r % t_new)

    s_past = jnp.concatenate([_dot(qbd, k_refs[j][...].astype(BF16)) for j in range(n_pages)], axis=1)
    kpos = lax.broadcasted_iota(jnp.int32, (1, past), 1)
    s_past = s_past - slope * (qpos - kpos).astype(F32)

    pad = jnp.zeros((page - t_new, D_QK), F32)
    kn = jnp.concatenate([kn_ref[...], pad], axis=0).astype(BF16)
    vn = jnp.concatenate([vn_ref[...], pad], axis=0).astype(BF16)
    d_new = qpos - (past + lax.broadcasted_iota(jnp.int32, (1, page), 1))
    s_new = jnp.where(d_new >= 0, _dot_nt(qbd, kn) - slope * d_new.astype(F32), NEG_INF)

    m = jnp.maximum(jnp.max(s_past, axis=-1, keepdims=True), jnp.max(s_new, axis=-1, keepdims=True))
    p_past = jnp.exp(s_past - m)
    p_new = jnp.exp(s_new - m)
    denom = jnp.sum(p_past, axis=-1, keepdims=True) + jnp.sum(p_new, axis=-1, keepdims=True)
    acc = _dot(p_new.astype(BF16), vn)
    for j in range(n_pages):
        v_page = jnp.concatenate([v_refs[j][pl.ds(h, page, stride=D_HEADS), :] for h in range(D_HEADS)], axis=1)
        acc = acc + _dot(p_past[:, j * page:(j + 1) * page].astype(BF16), v_page.astype(BF16))
    o = acc / denom

    lam = _diff_lambda(lam_ref, lam_init)
    g = g_ref[...]
    heads = []
    for h in range(D_HEADS):
        cols = slice(h * D_DV, (h + 1) * D_DV)
        o1 = o[(2 * h) * t_new:(2 * h + 1) * t_new, cols]
        o2 = o[(2 * h + 1) * t_new:(2 * h + 2) * t_new, cols]
        heads.append(_head_rms(o1 - lam * o2, g) * (1.0 - lam_init))
    o_ref[...] = jnp.concatenate(heads, axis=1)


def _diff_attn_sample(pa, kk, vv, cache_k, cache_v, page_table, lam_p, diff_g, *, layer, t_new, lam_init, name):
    db, n_pages = page_table.shape
    page = cache_k.shape[3]
    pt = page_table.reshape(-1)

    def page_spec(j, shape):
        return pl.BlockSpec((None, None) + shape, lambda b, pt_ref: (layer, pt_ref[b * n_pages + j], 0, 0))

    grid_spec = pltpu.PrefetchScalarGridSpec(
        num_scalar_prefetch=1,
        grid=(db,),
        in_specs=[pl.BlockSpec((t_new, D_QK), lambda b, pt_ref: (b, OFF_DQ // D_QK)),
                  pl.BlockSpec((t_new, D_QK), lambda b, pt_ref: (b, 0)),
                  pl.BlockSpec((t_new, D_V), lambda b, pt_ref: (b, 0))]
        + [page_spec(j, (D_QK, page)) for j in range(n_pages)]
        + [page_spec(j, (page * D_HEADS, D_DV)) for j in range(n_pages)]
        + [pl.BlockSpec((4, D_DK), lambda b, pt_ref: (0, 0)),
           pl.BlockSpec((1, D_DV), lambda b, pt_ref: (0, 0))],
        out_specs=pl.BlockSpec((t_new, D_V), lambda b, pt_ref: (b, 0)),
    )
    return pl.pallas_call(
        functools.partial(_dattn_s_body, n_pages=n_pages, page=page, t_new=t_new, lam_init=lam_init),
        grid_spec=grid_spec,
        out_shape=jax.ShapeDtypeStruct((db * t_new, D_V), F32),
        compiler_params=_params(1),
        name=name,
    )(pt, pa, kk, vv, *([cache_k] * n_pages), *([cache_v] * n_pages), lam_p, diff_g)


def _mix_body(ro_ref, do_ref, gr_ref, gd_ref, x_ref, wr_ref, wd_ref, wo_ref, g_ref, b_ref, o_ref, *, alpha):
    ret_branch = _dot(ro_ref[...].astype(BF16), wr_ref[...])
    diff_branch = _dot(do_ref[...].astype(BF16), wd_ref[...])
    merged = jax.nn.sigmoid(gr_ref[...]) * ret_branch + jax.nn.sigmoid(gd_ref[...]) * diff_branch
    mix = _dot(merged.astype(BF16), wo_ref[...])
    o_ref[...] = _layer_norm(alpha * x_ref[...] + mix, g_ref[...], b_ref[...])


def _mixer_out(ro, do, gates, x, w_ret_o, w_diff_o, w_out, ln_g, ln_b, *, layer, alpha, name):
    m = x.shape[0]
    tm = min(m, 512)
    row = lambda i: (i, 0)
    wspec = pl.BlockSpec((None, D_MODEL, D_MODEL), lambda i: (layer, 0, 0))
    lnspec = pl.BlockSpec((None, None, 1, D_MODEL), lambda i: (layer, 0, 0, 0))
    return pl.pallas_call(
        functools.partial(_mix_body, alpha=alpha),
        grid=(m // tm,),
        in_specs=[pl.BlockSpec((tm, R_V), row), pl.BlockSpec((tm, D_V), row),
                  pl.BlockSpec((tm, D_MODEL), lambda i: (i, 0)), pl.BlockSpec((tm, D_MODEL), lambda i: (i, 1)),
                  pl.BlockSpec((tm, D_MODEL), row), wspec, wspec, wspec, lnspec, lnspec],
        out_specs=pl.BlockSpec((tm, D_MODEL), row),
        out_shape=jax.ShapeDtypeStruct((m, D_MODEL), F32),
        compiler_params=_params(1),
        name=name,
    )(ro, do, gates, gates, x, w_ret_o, w_diff_o, w_out, ln_g, ln_b)


def _softmax_rows(s):
    m = jnp.max(s, axis=-1, keepdims=True)
    e = jnp.exp(s - m)
    return e / jnp.sum(e, axis=-1, keepdims=True)


def _mem_p_body(x_ref, mk_ref, mv_ref, wq_ref, wo_ref, g_ref, b_ref, o_ref, *, alpha):
    x = x_ref[...]
    q = _dot(x.astype(BF16), wq_ref[...])
    heads = []
    for h in range(M_HEADS):
        cols = slice(h * M_DH, (h + 1) * M_DH)
        s = _dot_nt(q[:, cols].astype(BF16), mk_ref[:, cols].astype(BF16)) * (M_DH ** -0.5)
        heads.append(_dot(_softmax_rows(s).astype(BF16), mv_ref[:, cols].astype(BF16)))
    o = jnp.concatenate(heads, axis=1)
    att = _dot(o.astype(BF16), wo_ref[...])
    o_ref[...] = _layer_norm(alpha * x + att, g_ref[...], b_ref[...])


def _mem_attn_prompt(x, mk, mv, w_q, w_o, ln_g, ln_b, *, layer, seq, alpha, name):
    m = x.shape[0]
    mem_len = mk.shape[0] // (m // seq)
    tm = min(seq, 512)
    per_batch = seq // tm
    row = lambda i: (i, 0)
    wspec = pl.BlockSpec((None, D_MODEL, D_MODEL), lambda i: (layer, 0, 0))
    lnspec = pl.BlockSpec((None, None, 1, D_MODEL), lambda i: (layer, 1, 0, 0))
    mspec = pl.BlockSpec((mem_len, D_MODEL), lambda i: (i // per_batch, 0))
    return pl.pallas_call(
        functools.partial(_mem_p_body, alpha=alpha),
        grid=(m // tm,),
        in_specs=[pl.BlockSpec((tm, D_MODEL), row), mspec, mspec, wspec, wspec, lnspec, lnspec],
        out_specs=pl.BlockSpec((tm, D_MODEL), row),
        out_shape=jax.ShapeDtypeStruct((m, D_MODEL), F32),
        compiler_params=_params(1),
        name=name,
    )(x, mk, mv, w_q, w_o, ln_g, ln_b)


def _mem_s_body(x_ref, mk_ref, mv_ref, wq_ref, wo_ref, g_ref, b_ref, o_ref, *, alpha, n_samples, t_new):
    x = x_ref[...]
    q = _dot(x.astype(BF16), wq_ref[...])
    outs = []
    mem_len = mk_ref.shape[1] // (D_MODEL // 128)
    halves = M_DH // 128

    def mem_rows(ref, n):
        return jnp.concatenate(
            [ref[n, pl.ds(j * M_HEADS + h, mem_len, stride=M_HEADS * halves), :]
             for h in range(M_HEADS) for j in range(halves)], axis=1).astype(BF16)

    for n in range(n_samples):
        qbd = _block_diag_rows(q[n * t_new:(n + 1) * t_new, :], M_HEADS, M_DH).astype(BF16)
        s = _dot_nt(qbd, mem_rows(mk_ref, n)) * (M_DH ** -0.5)
        o = _dot(_softmax_rows(s).astype(BF16), mem_rows(mv_ref, n))
        outs.append(jnp.concatenate(
            [o[h * t_new:(h + 1) * t_new, h * M_DH:(h + 1) * M_DH] for h in range(M_HEADS)], axis=1))
    o = jnp.concatenate(outs, axis=0)
    att = _dot(o.astype(BF16), wo_ref[...])
    o_ref[...] = _layer_norm(alpha * x + att, g_ref[...], b_ref[...])


def _mem_attn_sample(x, cache_mk, cache_mv, w_q, w_o, ln_g, ln_b, *, layer, t_new, alpha, name):
    m = x.shape[0]
    db, mem_rows = cache_mk.shape[1:3]
    ns = min(db, 4)
    rows = ns * t_new
    row = lambda i: (i, 0)
    wspec = pl.BlockSpec((None, D_MODEL, D_MODEL), lambda i: (layer, 0, 0))
    lnspec = pl.BlockSpec((None, None, 1, D_MODEL), lambda i: (layer, 1, 0, 0))
    mspec = pl.BlockSpec((None, ns, mem_rows, 128), lambda i: (layer, i, 0, 0))
    return pl.pallas_call(
        functools.partial(_mem_s_body, alpha=alpha, n_samples=ns, t_new=t_new),
        grid=(db // ns,),
        in_specs=[pl.BlockSpec((rows, D_MODEL), row), mspec, mspec, wspec, wspec, lnspec, lnspec],
        out_specs=pl.BlockSpec((rows, D_MODEL), row),
        out_shape=jax.ShapeDtypeStruct((m, D_MODEL), F32),
        compiler_params=_params(1),
        name=name,
    )(x, cache_mk, cache_mv, w_q, w_o, ln_g, ln_b)


FF_CHUNK = 256


def _ffn_body(x_ref, wgu_ref, wd_ref, g_ref, b_ref, o_ref, *, alpha):
    x = x_ref[...]
    xb = x.astype(BF16)
    y = alpha * x
    for c in range(D_FF // FF_CHUNK):
        gate = _dot(xb, wgu_ref[:, c * FF_CHUNK:(c + 1) * FF_CHUNK])
        up = _dot(xb, wgu_ref[:, D_FF + c * FF_CHUNK:D_FF + (c + 1) * FF_CHUNK])
        hidden = (gate * jax.nn.sigmoid(gate) * up).astype(BF16)
        y = y + _dot(hidden, wd_ref[c * FF_CHUNK:(c + 1) * FF_CHUNK, :])
    o_ref[...] = _layer_norm(y, g_ref[...], b_ref[...])


def _ffn(x, w_gu, w_down, ln_g, ln_b, *, layer, alpha, name):
    m = x.shape[0]
    tm = min(m, 512)
    row = lambda i: (i, 0)
    lnspec = pl.BlockSpec((None, None, 1, D_MODEL), lambda i: (layer, 2, 0, 0))
    return pl.pallas_call(
        functools.partial(_ffn_body, alpha=alpha),
        grid=(m // tm,),
        in_specs=[pl.BlockSpec((tm, D_MODEL), row),
                  pl.BlockSpec((None, D_MODEL, 2 * D_FF), lambda i: (layer, 0, 0)),
                  pl.BlockSpec((None, D_FF, D_MODEL), lambda i: (layer, 0, 0)),
                  lnspec, lnspec],
        out_specs=pl.BlockSpec((tm, D_MODEL), row),
        out_shape=jax.ShapeDtypeStruct((m, D_MODEL), F32),
        compiler_params=_params(1),
        name=name,
    )(x, w_gu, w_down, ln_g, ln_b)


def kernel(x_prompt, x_sample, mem_prompt, cache_k, cache_v, state_ret, cache_mem_k, cache_mem_v, page_table, w_in, w_ret_o, w_diff_o, w_out, ret_norm_g, diff_norm_g, diff_lambda, w_mem_q, w_mem_kv, w_mem_o, w_ffn_gu, w_ffn_down, ln_g, ln_b):
    depth = w_in.shape[0]
    batch, seq, _ = x_prompt.shape
    db, t_new, _ = x_sample.shape
    mem_len = mem_prompt.shape[1]
    n_pool, page = cache_k.shape[1:3]
    alpha = (2 * depth) ** 0.25

    w_in, w_ret_o, w_diff_o, w_out, w_mem_q, w_mem_kv, w_mem_o, w_ffn_gu, w_ffn_down = (
        w.astype(BF16) for w in (w_in, w_ret_o, w_diff_o, w_out, w_mem_q, w_mem_kv, w_mem_o, w_ffn_gu, w_ffn_down))
    cache_k = jnp.transpose(cache_k, (0, 1, 3, 4, 5, 2)).reshape(depth, n_pool, D_QK, page)
    cache_v = cache_v.reshape(depth, n_pool, page * D_HEADS, D_DV)

    def mem_view(c):
        c = c.reshape(depth, db, mem_len, M_HEADS, M_DH // 128, 128)
        return jnp.transpose(c, (0, 1, 2, 4, 3, 5)).reshape(depth, db, mem_len * (D_MODEL // 128), 128)

    cache_mem_k = mem_view(cache_mem_k)
    cache_mem_v = mem_view(cache_mem_v)
    ln_g = ln_g.reshape(depth, 3, 1, D_MODEL)
    ln_b = ln_b.reshape(depth, 3, 1, D_MODEL)
    mem = mem_prompt.reshape(batch * mem_len, D_MODEL)
    xp = x_prompt.reshape(batch * seq, D_MODEL)
    xs = x_sample.reshape(db * t_new, D_MODEL)

    keys_t = values_flat = states_p = states_s = None
    outs = [[] for _ in range(4)]
    for l in range(depth):
        lam_init = 0.8 - 0.6 * math.exp(-0.3 * l)
        ret_g = ret_norm_g[l].reshape(1, R_DV)
        diff_g = diff_norm_g[l].reshape(1, D_DV)
        lam_p = diff_lambda[l]
        wk_t = w_in[l, :, OFF_DK:OFF_DV].T

        mk_p = _matmul(mem, w_mem_kv, l, 0, D_MODEL, f"mem_k_{l}")
        mv_p = _matmul(mem, w_mem_kv, l, D_MODEL, D_MODEL, f"mem_v_{l}")

        pa = _matmul(xp, w_in, l, 0, OFF_DK, f"proj_a_p{l}")
        keys_t = _proj_keys_t(xp, wk_t, keys_t, layer=l, depth=depth, batch=batch, seq=seq, name=f"proj_k_p{l}")
        values_flat, vv = _proj_values(xp, w_in, values_flat, layer=l, depth=depth, name=f"proj_v_p{l}")
        gates = _matmul(xp, w_in, l, OFF_GR, 2 * D_MODEL, f"proj_g_p{l}")
        ro, states_p = _retention(pa, ret_g, None, states_p, layer=l, depth=depth, batch=batch, seq=seq,
                                  chunk=min(seq, 256), nb=1, out_dtype=BF16, name=f"retention_p{l}")
        do = _diff_attn_prompt(pa, keys_t, vv, lam_p, diff_g, layer=l, batch=batch, seq=seq, lam_init=lam_init,
                               name=f"diff_attn_p{l}")
        xp = _mixer_out(ro, do, gates, xp, w_ret_o, w_diff_o, w_out, ln_g, ln_b, layer=l, alpha=alpha,
                        name=f"mixer_out_p{l}")
        xp = _mem_attn_prompt(xp, mk_p, mv_p, w_mem_q, w_mem_o, ln_g, ln_b, layer=l, seq=seq, alpha=alpha,
                              name=f"mem_attn_p{l}")
        xp = _ffn(xp, w_ffn_gu, w_ffn_down, ln_g, ln_b, layer=l, alpha=alpha, name=f"ffn_p{l}")

        pa = _matmul(xs, w_in, l, 0, OFF_DK, f"proj_a_s{l}")
        k_s = _matmul(xs, w_in, l, OFF_DK, D_QK, f"proj_k_s{l}")
        v_s = _matmul(xs, w_in, l, OFF_DV, D_V, f"proj_v_s{l}")
        gates = _matmul(xs, w_in, l, OFF_GR, 2 * D_MODEL, f"proj_g_s{l}")
        ro, states_s = _retention(pa, ret_g, state_ret, states_s, layer=l, depth=depth, batch=db, seq=t_new,
                                  chunk=t_new, nb=min(db, 8), out_dtype=F32, name=f"retention_s{l}")
        do = _diff_attn_sample(pa, k_s, v_s, cache_k, cache_v, page_table, lam_p, diff_g, layer=l, t_new=t_new,
                               lam_init=lam_init, name=f"diff_attn_s{l}")
        xs = _mixer_out(ro, do, gates, xs, w_ret_o, w_diff_o, w_out, ln_g, ln_b, layer=l, alpha=alpha,
                        name=f"mixer_out_s{l}")
        xs = _mem_attn_sample(xs, cache_mem_k, cache_mem_v, w_mem_q, w_mem_o, ln_g, ln_b, layer=l, t_new=t_new,
                              alpha=alpha, name=f"mem_attn_s{l}")
        xs = _ffn(xs, w_ffn_gu, w_ffn_down, ln_g, ln_b, layer=l, alpha=alpha, name=f"ffn_s{l}")

        for lst, val in zip(outs, (
                k_s.reshape(db, t_new, D_HEADS, 2, D_DK), v_s.reshape(db, t_new, D_HEADS, D_DV),
                mk_p.reshape(batch, mem_len, M_HEADS, M_DH), mv_p.reshape(batch, mem_len, M_HEADS, M_DH))):
            lst.append(val)

    k_prompt = jnp.transpose(keys_t.reshape(depth, batch, D_HEADS, 2, D_DK, seq), (0, 1, 5, 2, 3, 4))
    v_prompt = values_flat.reshape(depth, batch, seq, D_HEADS, D_DV)
    k_sample, v_sample, mem_k, mem_v = (jnp.stack(o) for o in outs)
    return (xp.reshape(batch, seq, D_MODEL), xs.reshape(db, t_new, D_MODEL), k_prompt, v_prompt, k_sample, v_sample,
            states_p, states_s, mem_k, mem_v)
```

```python
import functools
import math

import jax
import jax.numpy as jnp
from jax import lax
from jax.experimental import pallas as pl
from jax.experimental.pallas import tpu as pltpu

F32 = jnp.float32
BF16 = jnp.bfloat16

D_MODEL = 1024
R_HEADS, R_DK, R_DV = 4, 128, 256
D_HEADS, D_DK = 8, 64
D_DV = 2 * D_DK
M_HEADS = 4
M_DH = D_MODEL // M_HEADS
D_FF = -(-8 * D_MODEL // (3 * 256)) * 256
LN_EPS = 1e-5
RMS_EPS = 1e-6
NEG_INF = -1e30

R_QK = R_HEADS * R_DK
R_V = R_HEADS * R_DV
D_QK = D_HEADS * 2 * D_DK
D_V = D_HEADS * D_DV
OFF_RQ = 0
OFF_RK = OFF_RQ + R_QK
OFF_RV = OFF_RK + R_QK
OFF_RG = OFF_RV + R_V
OFF_DQ = OFF_RG + R_V
OFF_DK = OFF_DQ + D_QK
OFF_DV = OFF_DK + D_QK
OFF_GR = OFF_DV + D_V
OFF_GD = OFF_GR + D_MODEL
IN_WIDTH = OFF_GD + D_MODEL

LOG_GAMMAS = tuple(math.log(1.0 - 2.0 ** (-5.0 - h)) for h in range(R_HEADS))
ALIBI_SLOPES = tuple(2.0 ** (-8.0 * (h + 1) / D_HEADS) for h in range(D_HEADS))

VMEM_LIMIT_BYTES = 56 * 1024 * 1024


def _params(n_grid_dims):
    return pltpu.CompilerParams(dimension_semantics=("arbitrary",) * n_grid_dims,
                                vmem_limit_bytes=VMEM_LIMIT_BYTES)


def _select_const(idx, values):
    out = jnp.float32(values[-1])
    for i in range(len(values) - 2, -1, -1):
        out = jnp.where(idx == i, jnp.float32(values[i]), out)
    return out


def _dot(a, b):
    return jnp.dot(a, b, preferred_element_type=F32)


def _dot_nt(a, b):
    return lax.dot_general(a, b, (((1,), (1,)), ((), ())), preferred_element_type=F32)


def _dot_tn(a, b):
    return lax.dot_general(a, b, (((0,), (0,)), ((), ())), preferred_element_type=F32)


def _layer_norm(y, g, b):
    mu = jnp.mean(y, axis=-1, keepdims=True)
    yc = y - mu
    var = jnp.mean(yc * yc, axis=-1, keepdims=True)
    return yc * lax.rsqrt(var + LN_EPS) * g + b


def _head_rms(o, g):
    return o * lax.rsqrt(jnp.mean(o * o, axis=-1, keepdims=True) + RMS_EPS) * g


def _diff_lambda(lam_ref, lam_init):
    lp = lam_ref[...]
    a = jnp.sum(lp[0:1, :] * lp[1:2, :], axis=-1, keepdims=True)
    b = jnp.sum(lp[2:3, :] * lp[3:4, :], axis=-1, keepdims=True)
    return jnp.exp(a) - jnp.exp(b) + lam_init


def _mm_body(x_ref, w_ref, o_ref, xb_ref):
    @pl.when(pl.program_id(1) == 0)
    def _():
        xb_ref[...] = x_ref[...].astype(BF16)

    o_ref[...] = _dot(xb_ref[...], w_ref[...])


def _matmul(x, w, layer, col0, ncols, name):
    m, k = x.shape
    tm = min(m, 2048)
    tn = min(ncols, 512)
    cb = col0 // tn
    return pl.pallas_call(
        _mm_body,
        grid=(m // tm, ncols // tn),
        in_specs=[pl.BlockSpec((tm, k), lambda i, j: (i, 0)),
                  pl.BlockSpec((None, k, tn), lambda i, j: (layer, 0, j + cb))],
        out_specs=pl.BlockSpec((tm, tn), lambda i, j: (i, j)),
        out_shape=jax.ShapeDtypeStruct((m, ncols), F32),
        scratch_shapes=[pltpu.VMEM((tm, k), BF16)],
        compiler_params=_params(2),
        name=name,
    )(x, w)


def _stacked_call(body, *, prev, layer, depth, per_layer_shape, stacked_spec, **kwargs):
    out_shape = kwargs.pop("out_shape")
    out_specs = kwargs.pop("out_specs")
    in_specs = kwargs.pop("in_specs")
    stacked = jax.ShapeDtypeStruct((depth,) + per_layer_shape, F32)
    aliases = {}
    extra = []
    if layer > 0:
        in_specs = list(in_specs) + [pl.BlockSpec(memory_space=pl.ANY)]
        aliases = {len(in_specs) - 1: 0}
        extra = [prev]

        def body_with_prev(*refs, _body=body, _n_in=len(in_specs)):
            return _body(*refs[:_n_in - 1], *refs[_n_in:])
        body = body_with_prev
    call = pl.pallas_call(body, in_specs=in_specs, out_specs=[stacked_spec] + list(out_specs),
                          out_shape=[stacked] + list(out_shape), input_output_aliases=aliases, **kwargs)
    return lambda *args: call(*args, *extra)


PROJ_TN = 512
_J_K = OFF_DK // PROJ_TN
_J_V = OFF_DV // PROJ_TN
_J_G = OFF_GR // PROJ_TN
_HEADS_PER_TILE = PROJ_TN // D_DV


def _proj_p_body(x_ref, w_ref, wkt_ref, kt_ref, vflat_ref, pa_ref, vv_ref, g_ref, xb_ref):
    j = pl.program_id(1)
    tm = x_ref.shape[0]

    @pl.when(j == 0)
    def _():
        xb_ref[...] = x_ref[...].astype(BF16)

    @pl.when(j < _J_K)
    def _():
        pa_ref[...] = _dot(xb_ref[...], w_ref[...])

    @pl.when((j >= _J_K) & (j < _J_V))
    def _():
        kt_ref[...] = _dot_nt(wkt_ref[...], xb_ref[...])

    @pl.when((j >= _J_V) & (j < _J_G))
    def _():
        v = _dot(xb_ref[...], w_ref[...])
        vv_ref[...] = v
        for hh in range(_HEADS_PER_TILE):
            head = (j - _J_V) * _HEADS_PER_TILE + hh
            vflat_ref[pl.ds(head, tm, stride=D_HEADS), :] = v[:, hh * D_DV:(hh + 1) * D_DV]

    @pl.when(j >= _J_G)
    def _():
        g_ref[...] = _dot(xb_ref[...], w_ref[...])


def _prompt_projection(x, w_in, wk_t, prev_keys_t, prev_vflat, *, layer, depth, batch, seq, name):
    m = x.shape[0]
    tm = min(seq, 1024)
    nsb = seq // tm
    tn = PROJ_TN
    n_j = IN_WIDTH // tn

    def clamp(j, lo, n):
        return jnp.clip(j - lo, 0, n - 1)

    in_specs = [pl.BlockSpec((tm, D_MODEL), lambda i, j: (i, 0)),
                pl.BlockSpec((None, D_MODEL, tn), lambda i, j: (layer, 0, jnp.where((j >= _J_K) & (j < _J_V), _J_K - 1, j))),
                pl.BlockSpec((tn, D_MODEL), lambda i, j: (clamp(j, _J_K, D_QK // tn), 0))]
    args = [x, w_in, wk_t]
    aliases = {}
    body = _proj_p_body
    if layer > 0:
        in_specs += [pl.BlockSpec(memory_space=pl.ANY)] * 2
        args += [prev_keys_t, prev_vflat]
        aliases = {3: 0, 4: 1}

        def body(x_ref, w_ref, wkt_ref, _prev_kt, _prev_vflat, *rest):
            return _proj_p_body(x_ref, w_ref, wkt_ref, *rest)
    return pl.pallas_call(
        body,
        grid=(m // tm, n_j),
        in_specs=in_specs,
        out_specs=[
            pl.BlockSpec((None, None, tn, tm), lambda i, j: (layer, i // nsb, clamp(j, _J_K, D_QK // tn), i % nsb)),
            pl.BlockSpec((None, tm * D_HEADS, D_DV), lambda i, j: (layer, i, 0)),
            pl.BlockSpec((tm, tn), lambda i, j: (i, clamp(j, 0, OFF_DK // tn))),
            pl.BlockSpec((tm, tn), lambda i, j: (i, clamp(j, _J_V, D_V // tn))),
            pl.BlockSpec((tm, tn), lambda i, j: (i, clamp(j, _J_G, 2 * D_MODEL // tn))),
        ],
        out_shape=[jax.ShapeDtypeStruct((depth, batch, D_QK, seq), F32),
                   jax.ShapeDtypeStruct((depth, m * D_HEADS, D_DV), F32),
                   jax.ShapeDtypeStruct((m, OFF_DK), F32),
                   jax.ShapeDtypeStruct((m, D_V), F32),
                   jax.ShapeDtypeStruct((m, 2 * D_MODEL), F32)],
        scratch_shapes=[pltpu.VMEM((tm, D_MODEL), BF16)],
        input_output_aliases=aliases,
        compiler_params=_params(2),
        name=name,
    )(*args)


def _ret_body(*refs, chunk, nb, has_s0):
    if has_s0:
        q_ref, k_ref, v_ref, rg_ref, g_ref, s0_ref, s_out_ref, o_ref, s_scr = refs
    else:
        q_ref, k_ref, v_ref, rg_ref, g_ref, s_out_ref, o_ref, s_scr = refs
    c = pl.program_id(1)

    @pl.when(c == 0)
    def _():
        if has_s0:
            s_scr[...] = s0_ref[...]
        else:
            s_scr[...] = jnp.zeros_like(s_scr)

    ii = lax.broadcasted_iota(jnp.int32, (chunk, chunk), 0)
    jj = lax.broadcasted_iota(jnp.int32, (chunk, chunk), 1)
    causal = ii >= jj
    dist = jnp.where(causal, (ii - jj).astype(F32), 0.0)
    pos = lax.broadcasted_iota(jnp.int32, (chunk, 1), 0).astype(F32)
    g = g_ref[...]

    for h in range(R_HEADS):
        lg = LOG_GAMMAS[h]
        decay = jnp.where(causal, jnp.exp(lg * dist), 0.0)
        q_decay = jnp.exp(lg * (pos + 1.0))
        k_decay = jnp.exp(lg * (chunk - 1.0 - pos))
        s_decay = math.exp(lg * chunk)
        qk_cols = slice(h * R_DK, (h + 1) * R_DK)
        v_cols = slice(h * R_DV, (h + 1) * R_DV)
        for n in range(nb):
            rows = slice(n * chunk, (n + 1) * chunk)
            q = q_ref[rows, qk_cols]
            k = k_ref[rows, qk_cols] * (R_DK ** -0.5)
            vb = v_ref[rows, v_cols].astype(BF16)
            s_prev = s_scr[n, h]
            inner = _dot_nt(q.astype(BF16), k.astype(BF16)) * decay
            o = _dot(inner.astype(BF16), vb) + _dot((q * q_decay).astype(BF16), s_prev.astype(BF16))
            s_scr[n, h] = s_decay * s_prev + _dot_tn((k * k_decay).astype(BF16), vb)
            rg = rg_ref[rows, v_cols]
            o_ref[rows, v_cols] = (_head_rms(o, g) * (rg * jax.nn.sigmoid(rg))).astype(o_ref.dtype)

    @pl.when(c == pl.num_programs(1) - 1)
    def _():
        s_out_ref[...] = s_scr[...]


def _retention(pa, ret_g, s0, prev_states, *, layer, depth, batch, seq, chunk, nb, out_dtype, name):
    nc = seq // chunk
    rows = nb * chunk
    has_s0 = s0 is not None

    def row_idx(b, c):
        return b * nc + c

    in_specs = [
        pl.BlockSpec((rows, R_QK), lambda b, c: (row_idx(b, c), OFF_RQ // R_QK)),
        pl.BlockSpec((rows, R_QK), lambda b, c: (row_idx(b, c), OFF_RK // R_QK)),
        pl.BlockSpec((rows, R_V), lambda b, c: (row_idx(b, c), OFF_RV // R_V)),
        pl.BlockSpec((rows, R_V), lambda b, c: (row_idx(b, c), OFF_RG // R_V)),
        pl.BlockSpec((1, R_DV), lambda b, c: (0, 0)),
    ]
    args = [pa, pa, pa, pa, ret_g]
    state_spec = pl.BlockSpec((None, nb, R_HEADS, R_DK, R_DV), lambda b, c: (layer, b, 0, 0, 0))
    if has_s0:
        in_specs.append(state_spec)
        args.append(s0)
    states, o = _stacked_call(
        functools.partial(_ret_body, chunk=chunk, nb=nb, has_s0=has_s0),
        prev=prev_states, layer=layer, depth=depth, per_layer_shape=(batch, R_HEADS, R_DK, R_DV),
        stacked_spec=state_spec,
        grid=(batch // nb, nc),
        in_specs=in_specs,
        out_specs=[pl.BlockSpec((rows, R_V), lambda b, c: (row_idx(b, c), 0))],
        out_shape=[jax.ShapeDtypeStruct((batch * seq, R_V), out_dtype)],
        scratch_shapes=[pltpu.VMEM((nb, R_HEADS, R_DK, R_DV), F32)],
        compiler_params=_params(2),
        name=name,
    )(*args)
    return o, states


ATTN_COLS = 256
ALIBI_RADIX = 256


LOG2_E = math.log2(math.e)
ACC_PAD = 16


def _dattn_p_body(q_ref, kt_ref, v_ref, lam_ref, g_ref, o_ref, kb_scr, vt_scr, qs_scr, sa_scr, sb_scr, pt_scr, m_scr,
                  a_scr, acc_scr, *, tq, tk, seq, lam_init):
    h = pl.program_id(1)
    qi = pl.program_id(2)
    slope = _select_const(h, ALIBI_SLOPES)
    n_kchunks = seq // tk

    @pl.when(qi == 0)
    def _():
        lane = lax.broadcasted_iota(jnp.int32, (tk, 128), 1)
        ones_row = jnp.where(lax.broadcasted_iota(jnp.int32, (ACC_PAD, tk), 0) == 0, 1.0, 0.0).astype(BF16)
        for j in range(n_kchunks):
            rows = slice(j * tk, (j + 1) * tk)
            kpos = j * tk + lax.broadcasted_iota(jnp.int32, (tk, 128), 0)
            aug = jnp.where(lane < 3, kpos // ALIBI_RADIX, jnp.where(lane < 6, kpos % ALIBI_RADIX, 0))
            kb_scr[rows, 0:D_DV] = kt_ref[:, rows].T.astype(BF16)
            kb_scr[rows, D_DV:2 * D_DV] = aug.astype(F32).astype(BF16)
            vt_scr[j, 0:D_DV, :] = v_ref[rows, :].T.astype(BF16)
            vt_scr[j, D_DV:D_DV + ACC_PAD, :] = ones_row

    q = q_ref[...] * (D_DK ** -0.5 * LOG2_E)
    lane = lax.broadcasted_iota(jnp.int32, q.shape, 1)
    qs_scr[0:tq, 0:D_DV] = jnp.where(lane < D_DK, q, 0.0).astype(BF16)
    qs_scr[tq:2 * tq, 0:D_DV] = jnp.where(lane >= D_DK, q, 0.0).astype(BF16)
    c = jnp.full((2 * tq, 128), slope * LOG2_E, F32)
    c1 = c.astype(BF16).astype(F32)
    c2 = (c - c1).astype(BF16).astype(F32)
    c3 = (c - c1 - c2).astype(BF16).astype(F32)
    lane2 = lax.broadcasted_iota(jnp.int32, (2 * tq, 128), 1)
    piece = jnp.where((lane2 == 0) | (lane2 == 3), c1, jnp.where((lane2 == 1) | (lane2 == 4), c2, c3))
    qs_scr[:, D_DV:2 * D_DV] = jnp.where(lane2 < 3, piece * ALIBI_RADIX,
                                         jnp.where(lane2 < 6, piece, 0.0)).astype(BF16)
    m_scr[...] = jnp.full_like(m_scr, NEG_INF)
    acc_scr[...] = jnp.zeros_like(acc_scr)

    def scores(ki, s_scr):
        k = kb_scr[pl.ds(pl.multiple_of(ki * tk, tk), tk), :]
        for c in range(2 * tq // ATTN_COLS):
            cols = slice(c * ATTN_COLS, (c + 1) * ATTN_COLS)
            s_scr[:, cols] = _dot_nt(k, qs_scr[cols, :])

    def softmax_pv(ki, s_scr, diag):
        for c in range(2 * tq // 128):
            cols = slice(c * 128, (c + 1) * 128)
            q0 = (c * 128) % tq
            if diag is not None and q0 + 127 < diag * tk:
                pt_scr[:, cols] = jnp.zeros((tk, 128), BF16)
                a_scr[:, cols] = jnp.ones((1, 128), F32)
                continue
            s = s_scr[:, cols]
            if diag is not None and q0 < diag * tk + tk - 1:
                kloc = diag * tk + lax.broadcasted_iota(jnp.int32, s.shape, 0)
                qloc = q0 + lax.broadcasted_iota(jnp.int32, s.shape, 1)
                s = jnp.where(kloc <= qloc, s, NEG_INF)
            m_prev = m_scr[:, cols]
            m_new = jnp.maximum(m_prev, jnp.max(s, axis=0, keepdims=True))
            m_scr[:, cols] = m_new
            a_scr[:, cols] = jnp.exp2(m_prev - m_new)
            pt_scr[:, cols] = jnp.exp2(s - m_new).astype(BF16)
        acc_scr[...] = acc_scr[...] * a_scr[...] + _dot(vt_scr[ki], pt_scr[...])

    n_diag = tq // tk
    assert n_diag == 2
    scores(0, sa_scr)

    def chunk_pair(j, carry):
        scores(2 * j + 1, sb_scr)
        softmax_pv(2 * j, sa_scr, None)
        scores(2 * j + 2, sa_scr)
        softmax_pv(2 * j + 1, sb_scr, None)
        return carry

    lax.fori_loop(0, qi, chunk_pair, 0)
    n_full = qi * n_diag
    scores(n_full + 1, sb_scr)
    softmax_pv(n_full, sa_scr, 0)
    softmax_pv(n_full + 1, sb_scr, 1)

    lam = _diff_lambda(lam_ref, lam_init)
    o = acc_scr[0:D_DV, :] * (1.0 / acc_scr[D_DV:D_DV + 1, :])
    o = o[:, 0:tq] - lam * o[:, tq:2 * tq]
    g = jnp.concatenate([g_ref[...]] * (tq // 128), axis=1)
    o = o * lax.rsqrt(jnp.mean(o * o, axis=0, keepdims=True) + RMS_EPS) * g * (1.0 - lam_init)
    o_ref[...] = o.T.astype(o_ref.dtype)


def _diff_attn_prompt(pa, keys_t, vv, lam_p, diff_g, *, layer, batch, seq, lam_init, name):
    tq = min(seq, 512)
    tk = tq // 2
    nq = seq // tq
    assert all(s == 2.0 ** round(math.log2(s)) for s in ALIBI_SLOPES) and seq <= ALIBI_RADIX * ALIBI_RADIX
    g_tile = jnp.broadcast_to(diff_g.reshape(D_DV, 1), (D_DV, 128))
    return pl.pallas_call(
        functools.partial(_dattn_p_body, tq=tq, tk=tk, seq=seq, lam_init=lam_init),
        grid=(batch, D_HEADS, nq),
        in_specs=[
            pl.BlockSpec((tq, D_DV), lambda b, h, qi: (b * nq + qi, OFF_DQ // D_DV + h)),
            pl.BlockSpec((None, None, D_DV, seq), lambda b, h, qi: (layer, b, h, 0)),
            pl.BlockSpec((seq, D_DV), lambda b, h, qi: (b, h)),
            pl.BlockSpec((4, D_DK), lambda b, h, qi: (0, 0)),
            pl.BlockSpec((D_DV, 128), lambda b, h, qi: (0, 0)),
        ],
        out_specs=pl.BlockSpec((tq, D_DV), lambda b, h, qi: (b * nq + qi, h)),
        out_shape=jax.ShapeDtypeStruct((batch * seq, D_V), BF16),
        scratch_shapes=[
            pltpu.VMEM((seq, 2 * D_DV), BF16),
            pltpu.VMEM((seq // tk, D_DV + ACC_PAD, tk), BF16),
            pltpu.VMEM((2 * tq, 2 * D_DV), BF16),
            pltpu.VMEM((tk, 2 * tq), F32),
            pltpu.VMEM((tk, 2 * tq), F32),
            pltpu.VMEM((tk, 2 * tq), BF16),
            pltpu.VMEM((1, 2 * tq), F32),
            pltpu.VMEM((1, 2 * tq), F32),
            pltpu.VMEM((D_DV + ACC_PAD, 2 * tq), F32),
        ],
        compiler_params=_params(3),
        name=name,
    )(pa, keys_t, vv, lam_p, g_tile)


def _block_diag_rows(q, n_groups, group_width):
    t = q.shape[0]
    qt = jnp.concatenate([q] * n_groups, axis=0)
    rg = lax.broadcasted_iota(jnp.int32, qt.shape, 0) // t
    cg = lax.broadcasted_iota(jnp.int32, qt.shape, 1) // group_width
    return jnp.where(rg == cg, qt, 0.0)


def _dattn_s_body(pt_ref, q_ref, kn_ref, vn_ref, *rest, n_pages, page, t_new, lam_init):
    k_refs = rest[:n_pages]
    v_refs = rest[n_pages:2 * n_pages]
    lam_ref, g_ref, o_ref = rest[2 * n_pages:]
    past = n_pages * page
    n_rows = 2 * D_HEADS * t_new

    qbd = _block_diag_rows(q_ref[...] * (D_DK ** -0.5), 2 * D_HEADS, D_DK).astype(BF16)
    r = lax.broadcasted_iota(jnp.int32, (n_rows, 1), 0)
    slope = jnp.exp2(-8.0 * ((r // (2 * t_new)) + 1).astype(F32) / D_HEADS)
    qpos = past + (r % t_new)

    s_past = jnp.concatenate([_dot(qbd, k_refs[j][...].astype(BF16)) for j in range(n_pages)], axis=1)
    kpos = lax.broadcasted_iota(jnp.int32, (1, past), 1)
    s_past = s_past - slope * (qpos - kpos).astype(F32)

    pad = jnp.zeros((page - t_new, D_QK), F32)
    kn = jnp.concatenate([kn_ref[...], pad], axis=0).astype(BF16)
    vn = jnp.concatenate([vn_ref[...], pad], axis=0).astype(BF16)
    d_new = qpos - (past + lax.broadcasted_iota(jnp.int32, (1, page), 1))
    s_new = jnp.where(d_new >= 0, _dot_nt(qbd, kn) - slope * d_new.astype(F32), NEG_INF)

    m = jnp.maximum(jnp.max(s_past, axis=-1, keepdims=True), jnp.max(s_new, axis=-1, keepdims=True))
    p_past = jnp.exp(s_past - m)
    p_new = jnp.exp(s_new - m)
    denom = jnp.sum(p_past, axis=-1, keepdims=True) + jnp.sum(p_new, axis=-1, keepdims=True)
    acc = _dot(p_new.astype(BF16), vn)
    for j in range(n_pages):
        v_page = jnp.concatenate([v_refs[j][pl.ds(h, page, stride=D_HEADS), :] for h in range(D_HEADS)], axis=1)
        acc = acc + _dot(p_past[:, j * page:(j + 1) * page].astype(BF16), v_page.astype(BF16))
    o = acc / denom

    lam = _diff_lambda(lam_ref, lam_init)
    g = g_ref[...]
    heads = []
    for h in range(D_HEADS):
        cols = slice(h * D_DV, (h + 1) * D_DV)
        o1 = o[(2 * h) * t_new:(2 * h + 1) * t_new, cols]
        o2 = o[(2 * h + 1) * t_new:(2 * h + 2) * t_new, cols]
        heads.append(_head_rms(o1 - lam * o2, g) * (1.0 - lam_init))
    o_ref[...] = jnp.concatenate(heads, axis=1)


def _diff_attn_sample(pa, kk, vv, cache_k, cache_v, page_table, lam_p, diff_g, *, layer, t_new, lam_init, name):
    db, n_pages = page_table.shape
    page = cache_k.shape[3]
    pt = page_table.reshape(-1)

    def page_spec(j, shape):
        return pl.BlockSpec((None, None) + shape, lambda b, pt_ref: (layer, pt_ref[b * n_pages + j], 0, 0))

    grid_spec = pltpu.PrefetchScalarGridSpec(
        num_scalar_prefetch=1,
        grid=(db,),
        in_specs=[pl.BlockSpec((t_new, D_QK), lambda b, pt_ref: (b, OFF_DQ // D_QK)),
                  pl.BlockSpec((t_new, D_QK), lambda b, pt_ref: (b, 0)),
                  pl.BlockSpec((t_new, D_V), lambda b, pt_ref: (b, 0))]
        + [page_spec(j, (D_QK, page)) for j in range(n_pages)]
        + [page_spec(j, (page * D_HEADS, D_DV)) for j in range(n_pages)]
        + [pl.BlockSpec((4, D_DK), lambda b, pt_ref: (0, 0)),
           pl.BlockSpec((1, D_DV), lambda b, pt_ref: (0, 0))],
        out_specs=pl.BlockSpec((t_new, D_V), lambda b, pt_ref: (b, 0)),
    )
    return pl.pallas_call(
        functools.partial(_dattn_s_body, n_pages=n_pages, page=page, t_new=t_new, lam_init=lam_init),
        grid_spec=grid_spec,
        out_shape=jax.ShapeDtypeStruct((db * t_new, D_V), F32),
        compiler_params=_params(1),
        name=name,
    )(pt, pa, kk, vv, *([cache_k] * n_pages), *([cache_v] * n_pages), lam_p, diff_g)


def _mix_body(ro_ref, do_ref, gr_ref, gd_ref, x_ref, wr_ref, wd_ref, wo_ref, g_ref, b_ref, o_ref, *, alpha):
    ret_branch = _dot(ro_ref[...].astype(BF16), wr_ref[...])
    diff_branch = _dot(do_ref[...].astype(BF16), wd_ref[...])
    merged = jax.nn.sigmoid(gr_ref[...]) * ret_branch + jax.nn.sigmoid(gd_ref[...]) * diff_branch
    mix = _dot(merged.astype(BF16), wo_ref[...])
    o_ref[...] = _layer_norm(alpha * x_ref[...] + mix, g_ref[...], b_ref[...])


def _mixer_out(ro, do, gates, x, w_ret_o, w_diff_o, w_out, ln_g, ln_b, *, layer, alpha, name):
    m = x.shape[0]
    tm = min(m, 512)
    row = lambda i: (i, 0)
    wspec = pl.BlockSpec((None, D_MODEL, D_MODEL), lambda i: (layer, 0, 0))
    lnspec = pl.BlockSpec((None, None, 1, D_MODEL), lambda i: (layer, 0, 0, 0))
    return pl.pallas_call(
        functools.partial(_mix_body, alpha=alpha),
        grid=(m // tm,),
        in_specs=[pl.BlockSpec((tm, R_V), row), pl.BlockSpec((tm, D_V), row),
                  pl.BlockSpec((tm, D_MODEL), lambda i: (i, 0)), pl.BlockSpec((tm, D_MODEL), lambda i: (i, 1)),
                  pl.BlockSpec((tm, D_MODEL), row), wspec, wspec, wspec, lnspec, lnspec],
        out_specs=pl.BlockSpec((tm, D_MODEL), row),
        out_shape=jax.ShapeDtypeStruct((m, D_MODEL), F32),
        compiler_params=_params(1),
        name=name,
    )(ro, do, gates, gates, x, w_ret_o, w_diff_o, w_out, ln_g, ln_b)


def _softmax_rows(s):
    m = jnp.max(s, axis=-1, keepdims=True)
    e = jnp.exp(s - m)
    return e / jnp.sum(e, axis=-1, keepdims=True)


def _mem_p_body(x_ref, mk_ref, mv_ref, wq_ref, wo_ref, g_ref, b_ref, o_ref, *, alpha):
    x = x_ref[...]
    q = _dot(x.astype(BF16), wq_ref[...])
    heads = []
    for h in range(M_HEADS):
        cols = slice(h * M_DH, (h + 1) * M_DH)
        s = _dot_nt(q[:, cols].astype(BF16), mk_ref[:, cols].astype(BF16)) * (M_DH ** -0.5)
        heads.append(_dot(_softmax_rows(s).astype(BF16), mv_ref[:, cols].astype(BF16)))
    o = jnp.concatenate(heads, axis=1)
    att = _dot(o.astype(BF16), wo_ref[...])
    o_ref[...] = _layer_norm(alpha * x + att, g_ref[...], b_ref[...])


def _mem_attn_prompt(x, mk, mv, w_q, w_o, ln_g, ln_b, *, layer, seq, alpha, name):
    m = x.shape[0]
    mem_len = mk.shape[0] // (m // seq)
    tm = min(seq, 512)
    per_batch = seq // tm
    row = lambda i: (i, 0)
    wspec = pl.BlockSpec((None, D_MODEL, D_MODEL), lambda i: (layer, 0, 0))
    lnspec = pl.BlockSpec((None, None, 1, D_MODEL), lambda i: (layer, 1, 0, 0))
    mspec = pl.BlockSpec((mem_len, D_MODEL), lambda i: (i // per_batch, 0))
    return pl.pallas_call(
        functools.partial(_mem_p_body, alpha=alpha),
        grid=(m // tm,),
        in_specs=[pl.BlockSpec((tm, D_MODEL), row), mspec, mspec, wspec, wspec, lnspec, lnspec],
        out_specs=pl.BlockSpec((tm, D_MODEL), row),
        out_shape=jax.ShapeDtypeStruct((m, D_MODEL), F32),
        compiler_params=_params(1),
        name=name,
    )(x, mk, mv, w_q, w_o, ln_g, ln_b)


def _mem_s_body(x_ref, mk_ref, mv_ref, wq_ref, wo_ref, g_ref, b_ref, o_ref, *, alpha, n_samples, t_new):
    x = x_ref[...]
    q = _dot(x.astype(BF16), wq_ref[...])
    outs = []
    mem_len = mk_ref.shape[1] // (D_MODEL // 128)
    halves = M_DH // 128

    def mem_rows(ref, n):
        return jnp.concatenate(
            [ref[n, pl.ds(j * M_HEADS + h, mem_len, stride=M_HEADS * halves), :]
             for h in range(M_HEADS) for j in range(halves)], axis=1).astype(BF16)

    for n in range(n_samples):
        qbd = _block_diag_rows(q[n * t_new:(n + 1) * t_new, :], M_HEADS, M_DH).astype(BF16)
        s = _dot_nt(qbd, mem_rows(mk_ref, n)) * (M_DH ** -0.5)
        o = _dot(_softmax_rows(s).astype(BF16), mem_rows(mv_ref, n))
        outs.append(jnp.concatenate(
            [o[h * t_new:(h + 1) * t_new, h * M_DH:(h + 1) * M_DH] for h in range(M_HEADS)], axis=1))
    o = jnp.concatenate(outs, axis=0)
    att = _dot(o.astype(BF16), wo_ref[...])
    o_ref[...] = _layer_norm(alpha * x + att, g_ref[...], b_ref[...])


def _mem_attn_sample(x, cache_mk, cache_mv, w_q, w_o, ln_g, ln_b, *, layer, t_new, alpha, name):
    m = x.shape[0]
    db, mem_rows = cache_mk.shape[1:3]
    ns = min(db, 4)
    rows = ns * t_new
    row = lambda i: (i, 0)
    wspec = pl.BlockSpec((None, D_MODEL, D_MODEL), lambda i: (layer, 0, 0))
    lnspec = pl.BlockSpec((None, None, 1, D_MODEL), lambda i: (layer, 1, 0, 0))
    mspec = pl.BlockSpec((None, ns, mem_rows, 128), lambda i: (layer, i, 0, 0))
    return pl.pallas_call(
        functools.partial(_mem_s_body, alpha=alpha, n_samples=ns, t_new=t_new),
        grid=(db // ns,),
        in_specs=[pl.BlockSpec((rows, D_MODEL), row), mspec, mspec, wspec, wspec, lnspec, lnspec],
        out_specs=pl.BlockSpec((rows, D_MODEL), row),
        out_shape=jax.ShapeDtypeStruct((m, D_MODEL), F32),
        compiler_params=_params(1),
        name=name,
    )(x, cache_mk, cache_mv, w_q, w_o, ln_g, ln_b)


FF_CHUNK = 256


def _ffn_body(x_ref, wgu_ref, wd_ref, g_ref, b_ref, o_ref, *, alpha):
    x = x_ref[...]
    xb = x.astype(BF16)
    y = alpha * x
    for c in range(D_FF // FF_CHUNK):
        gate = _dot(xb, wgu_ref[:, c * FF_CHUNK:(c + 1) * FF_CHUNK])
        up = _dot(xb, wgu_ref[:, D_FF + c * FF_CHUNK:D_FF + (c + 1) * FF_CHUNK])
        hidden = (gate * jax.nn.sigmoid(gate) * up).astype(BF16)
        y = y + _dot(hidden, wd_ref[c * FF_CHUNK:(c + 1) * FF_CHUNK, :])
    o_ref[...] = _layer_norm(y, g_ref[...], b_ref[...])


def _ffn(x, w_gu, w_down, ln_g, ln_b, *, layer, alpha, name):
    m = x.shape[0]
    tm = min(m, 512)
    row = lambda i: (i, 0)
    lnspec = pl.BlockSpec((None, None, 1, D_MODEL), lambda i: (layer, 2, 0, 0))
    return pl.pallas_call(
        functools.partial(_ffn_body, alpha=alpha),
        grid=(m // tm,),
        in_specs=[pl.BlockSpec((tm, D_MODEL), row),
                  pl.BlockSpec((None, D_MODEL, 2 * D_FF), lambda i: (layer, 0, 0)),
                  pl.BlockSpec((None, D_FF, D_MODEL), lambda i: (layer, 0, 0)),
                  lnspec, lnspec],
        out_specs=pl.BlockSpec((tm, D_MODEL), row),
        out_shape=jax.ShapeDtypeStruct((m, D_MODEL), F32),
        compiler_params=_params(1),
        name=name,
    )(x, w_gu, w_down, ln_g, ln_b)


def kernel(x_prompt, x_sample, mem_prompt, cache_k, cache_v, state_ret, cache_mem_k, cache_mem_v, page_table, w_in, w_ret_o, w_diff_o, w_out, ret_norm_g, diff_norm_g, diff_lambda, w_mem_q, w_mem_kv, w_mem_o, w_ffn_gu, w_ffn_down, ln_g, ln_b):
    depth = w_in.shape[0]
    batch, seq, _ = x_prompt.shape
    db, t_new, _ = x_sample.shape
    mem_len = mem_prompt.shape[1]
    n_pool, page = cache_k.shape[1:3]
    alpha = (2 * depth) ** 0.25

    w_in, w_ret_o, w_diff_o, w_out, w_mem_q, w_mem_kv, w_mem_o, w_ffn_gu, w_ffn_down = (
        w.astype(BF16) for w in (w_in, w_ret_o, w_diff_o, w_out, w_mem_q, w_mem_kv, w_mem_o, w_ffn_gu, w_ffn_down))
    cache_k = jnp.transpose(cache_k, (0, 1, 3, 4, 5, 2)).reshape(depth, n_pool, D_QK, page)
    cache_v = cache_v.reshape(depth, n_pool, page * D_HEADS, D_DV)

    def mem_view(c):
        c = c.reshape(depth, db, mem_len, M_HEADS, M_DH // 128, 128)
        return jnp.transpose(c, (0, 1, 2, 4, 3, 5)).reshape(depth, db, mem_len * (D_MODEL // 128), 128)

    cache_mem_k = mem_view(cache_mem_k)
    cache_mem_v = mem_view(cache_mem_v)
    ln_g = ln_g.reshape(depth, 3, 1, D_MODEL)
    ln_b = ln_b.reshape(depth, 3, 1, D_MODEL)
    mem = mem_prompt.reshape(batch * mem_len, D_MODEL)
    xp = x_prompt.reshape(batch * seq, D_MODEL)
    xs = x_sample.reshape(db * t_new, D_MODEL)

    keys_t = values_flat = states_p = states_s = None
    outs = [[] for _ in range(4)]
    for l in range(depth):
        lam_init = 0.8 - 0.6 * math.exp(-0.3 * l)
        ret_g = ret_norm_g[l].reshape(1, R_DV)
        diff_g = diff_norm_g[l].reshape(1, D_DV)
        lam_p = diff_lambda[l]
        wk_t = w_in[l, :, OFF_DK:OFF_DV].T

        mk_p = _matmul(mem, w_mem_kv, l, 0, D_MODEL, f"mem_k_{l}")
        mv_p = _matmul(mem, w_mem_kv, l, D_MODEL, D_MODEL, f"mem_v_{l}")

        keys_t, values_flat, pa, vv, gates = _prompt_projection(
            xp, w_in, wk_t, keys_t, values_flat, layer=l, depth=depth, batch=batch, seq=seq, name=f"proj_p{l}")
        ro, states_p = _retention(pa, ret_g, None, states_p, layer=l, depth=depth, batch=batch, seq=seq,
                                  chunk=min(seq, 256), nb=1, out_dtype=BF16, name=f"retention_p{l}")
        do = _diff_attn_prompt(pa, keys_t, vv, lam_p, diff_g, layer=l, batch=batch, seq=seq, lam_init=lam_init,
                               name=f"diff_attn_p{l}")
        xp = _mixer_out(ro, do, gates, xp, w_ret_o, w_diff_o, w_out, ln_g, ln_b, layer=l, alpha=alpha,
                        name=f"mixer_out_p{l}")
        xp = _mem_attn_prompt(xp, mk_p, mv_p, w_mem_q, w_mem_o, ln_g, ln_b, layer=l, seq=seq, alpha=alpha,
                              name=f"mem_attn_p{l}")
        xp = _ffn(xp, w_ffn_gu, w_ffn_down, ln_g, ln_b, layer=l, alpha=alpha, name=f"ffn_p{l}")

        pa = _matmul(xs, w_in, l, 0, OFF_DK, f"proj_a_s{l}")
        k_s = _matmul(xs, w_in, l, OFF_DK, D_QK, f"proj_k_s{l}")
        v_s = _matmul(xs, w_in, l, OFF_DV, D_V, f"proj_v_s{l}")
        gates = _matmul(xs, w_in, l, OFF_GR, 2 * D_MODEL, f"proj_g_s{l}")
        ro, states_s = _retention(pa, ret_g, state_ret, states_s, layer=l, depth=depth, batch=db, seq=t_new,
                                  chunk=t_new, nb=min(db, 8), out_dtype=F32, name=f"retention_s{l}")
        do = _diff_attn_sample(pa, k_s, v_s, cache_k, cache_v, page_table, lam_p, diff_g, layer=l, t_new=t_new,
                               lam_init=lam_init, name=f"diff_attn_s{l}")
        xs = _mixer_out(ro, do, gates, xs, w_ret_o, w_diff_o, w_out, ln_g, ln_b, layer=l, alpha=alpha,
                        name=f"mixer_out_s{l}")
        xs = _mem_attn_sample(xs, cache_mem_k, cache_mem_v, w_mem_q, w_mem_o, ln_g, ln_b, layer=l, t_new=t_new,
                              alpha=alpha, name=f"mem_attn_s{l}")
        xs = _ffn(xs, w_ffn_gu, w_ffn_down, ln_g, ln_b, layer=l, alpha=alpha, name=f"ffn_s{l}")

        for lst, val in zip(outs, (
                k_s.reshape(db, t_new, D_HEADS, 2, D_DK), v_s.reshape(db, t_new, D_HEADS, D_DV),
                mk_p.reshape(batch, mem_len, M_HEADS, M_DH), mv_p.reshape(batch, mem_len, M_HEADS, M_DH))):
            lst.append(val)

    k_prompt = jnp.transpose(keys_t.reshape(depth, batch, D_HEADS, 2, D_DK, seq), (0, 1, 5, 2, 3, 4))
    v_prompt = values_flat.reshape(depth, batch, seq, D_HEADS, D_DV)
    k_sample, v_sample, mem_k, mem_v = (jnp.stack(o) for o in outs)
    return (xp.reshape(batch, seq, D_MODEL), xs.reshape(db, t_new, D_MODEL), k_prompt, v_prompt, k_sample, v_sample,
            states_p, states_s, mem_k, mem_v)
```

```python
import functools
import math

import jax
import jax.numpy as jnp
from jax import lax
from jax.experimental import pallas as pl
from jax.experimental.pallas import tpu as pltpu

F32 = jnp.float32
BF16 = jnp.bfloat16

D_MODEL = 1024
R_HEADS, R_DK, R_DV = 4, 128, 256
D_HEADS, D_DK = 8, 64
D_DV = 2 * D_DK
M_HEADS = 4
M_DH = D_MODEL // M_HEADS
D_FF = -(-8 * D_MODEL // (3 * 256)) * 256
LN_EPS = 1e-5
RMS_EPS = 1e-6
NEG_INF = -1e30

R_QK = R_HEADS * R_DK
R_V = R_HEADS * R_DV
D_QK = D_HEADS * 2 * D_DK
D_V = D_HEADS * D_DV
OFF_RQ = 0
OFF_RK = OFF_RQ + R_QK
OFF_RV = OFF_RK + R_QK
OFF_RG = OFF_RV + R_V
OFF_DQ = OFF_RG + R_V
OFF_DK = OFF_DQ + D_QK
OFF_DV = OFF_DK + D_QK
OFF_GR = OFF_DV + D_V
OFF_GD = OFF_GR + D_MODEL
IN_WIDTH = OFF_GD + D_MODEL

LOG_GAMMAS = tuple(math.log(1.0 - 2.0 ** (-5.0 - h)) for h in range(R_HEADS))
ALIBI_SLOPES = tuple(2.0 ** (-8.0 * (h + 1) / D_HEADS) for h in range(D_HEADS))

VMEM_LIMIT_BYTES = 56 * 1024 * 1024


def _params(n_grid_dims):
    return pltpu.CompilerParams(dimension_semantics=("arbitrary",) * n_grid_dims,
                                vmem_limit_bytes=VMEM_LIMIT_BYTES)


def _select_const(idx, values):
    out = jnp.float32(values[-1])
    for i in range(len(values) - 2, -1, -1):
        out = jnp.where(idx == i, jnp.float32(values[i]), out)
    return out


def _dot(a, b):
    return jnp.dot(a, b, preferred_element_type=F32)


def _dot_nt(a, b):
    return lax.dot_general(a, b, (((1,), (1,)), ((), ())), preferred_element_type=F32)


def _dot_tn(a, b):
    return lax.dot_general(a, b, (((0,), (0,)), ((), ())), preferred_element_type=F32)


def _layer_norm(y, g, b):
    mu = jnp.mean(y, axis=-1, keepdims=True)
    yc = y - mu
    var = jnp.mean(yc * yc, axis=-1, keepdims=True)
    return yc * lax.rsqrt(var + LN_EPS) * g + b


def _head_rms(o, g):
    return o * lax.rsqrt(jnp.mean(o * o, axis=-1, keepdims=True) + RMS_EPS) * g


def _diff_lambda(lam_ref, lam_init):
    lp = lam_ref[...]
    a = jnp.sum(lp[0:1, :] * lp[1:2, :], axis=-1, keepdims=True)
    b = jnp.sum(lp[2:3, :] * lp[3:4, :], axis=-1, keepdims=True)
    return jnp.exp(a) - jnp.exp(b) + lam_init


def _mm_body(x_ref, w_ref, o_ref, xb_ref):
    @pl.when(pl.program_id(1) == 0)
    def _():
        xb_ref[...] = x_ref[...].astype(BF16)

    o_ref[...] = _dot(xb_ref[...], w_ref[...])


def _matmul(x, w, layer, col0, ncols, name):
    m, k = x.shape
    tm = min(m, 2048)
    tn = min(ncols, 512)
    cb = col0 // tn
    return pl.pallas_call(
        _mm_body,
        grid=(m // tm, ncols // tn),
        in_specs=[pl.BlockSpec((tm, k), lambda i, j: (i, 0)),
                  pl.BlockSpec((None, k, tn), lambda i, j: (layer, 0, j + cb))],
        out_specs=pl.BlockSpec((tm, tn), lambda i, j: (i, j)),
        out_shape=jax.ShapeDtypeStruct((m, ncols), F32),
        scratch_shapes=[pltpu.VMEM((tm, k), BF16)],
        compiler_params=_params(2),
        name=name,
    )(x, w)


def _stacked_call(body, *, prev, layer, depth, per_layer_shape, stacked_spec, **kwargs):
    out_shape = kwargs.pop("out_shape")
    out_specs = kwargs.pop("out_specs")
    in_specs = kwargs.pop("in_specs")
    stacked = jax.ShapeDtypeStruct((depth,) + per_layer_shape, F32)
    aliases = {}
    extra = []
    if layer > 0:
        in_specs = list(in_specs) + [pl.BlockSpec(memory_space=pl.ANY)]
        aliases = {len(in_specs) - 1: 0}
        extra = [prev]

        def body_with_prev(*refs, _body=body, _n_in=len(in_specs)):
            return _body(*refs[:_n_in - 1], *refs[_n_in:])
        body = body_with_prev
    call = pl.pallas_call(body, in_specs=in_specs, out_specs=[stacked_spec] + list(out_specs),
                          out_shape=[stacked] + list(out_shape), input_output_aliases=aliases, **kwargs)
    return lambda *args: call(*args, *extra)


PROJ_TN = 512
_J_K = OFF_DK // PROJ_TN
_J_V = OFF_DV // PROJ_TN
_J_G = OFF_GR // PROJ_TN
_HEADS_PER_TILE = PROJ_TN // D_DV


def _proj_p_body(x_ref, w_ref, wkt_ref, kt_ref, vflat_ref, pa_ref, vv_ref, g_ref, xb_ref):
    j = pl.program_id(1)
    tm = x_ref.shape[0]

    @pl.when(j == 0)
    def _():
        xb_ref[...] = x_ref[...].astype(BF16)

    @pl.when(j < _J_K)
    def _():
        pa_ref[...] = _dot(xb_ref[...], w_ref[...])

    @pl.when((j >= _J_K) & (j < _J_V))
    def _():
        kt_ref[...] = _dot_nt(wkt_ref[...], xb_ref[...])

    @pl.when((j >= _J_V) & (j < _J_G))
    def _():
        v = _dot(xb_ref[...], w_ref[...])
        vv_ref[...] = v
        for hh in range(_HEADS_PER_TILE):
            head = (j - _J_V) * _HEADS_PER_TILE + hh
            vflat_ref[pl.ds(head, tm, stride=D_HEADS), :] = v[:, hh * D_DV:(hh + 1) * D_DV]

    @pl.when(j >= _J_G)
    def _():
        g_ref[...] = _dot(xb_ref[...], w_ref[...])


def _prompt_projection(x, w_in, wk_t, prev_keys_t, prev_vflat, *, layer, depth, batch, seq, name):
    m = x.shape[0]
    tm = min(seq, 1024)
    nsb = seq // tm
    tn = PROJ_TN
    n_j = IN_WIDTH // tn

    def clamp(j, lo, n):
        return jnp.clip(j - lo, 0, n - 1)

    in_specs = [pl.BlockSpec((tm, D_MODEL), lambda i, j: (i, 0)),
                pl.BlockSpec((None, D_MODEL, tn), lambda i, j: (layer, 0, jnp.where((j >= _J_K) & (j < _J_V), _J_K - 1, j))),
                pl.BlockSpec((tn, D_MODEL), lambda i, j: (clamp(j, _J_K, D_QK // tn), 0))]
    args = [x, w_in, wk_t]
    aliases = {}
    body = _proj_p_body
    if layer > 0:
        in_specs += [pl.BlockSpec(memory_space=pl.ANY)] * 2
        args += [prev_keys_t, prev_vflat]
        aliases = {3: 0, 4: 1}

        def body(x_ref, w_ref, wkt_ref, _prev_kt, _prev_vflat, *rest):
            return _proj_p_body(x_ref, w_ref, wkt_ref, *rest)
    return pl.pallas_call(
        body,
        grid=(m // tm, n_j),
        in_specs=in_specs,
        out_specs=[
            pl.BlockSpec((None, None, tn, tm), lambda i, j: (layer, i // nsb, clamp(j, _J_K, D_QK // tn), i % nsb)),
            pl.BlockSpec((None, tm * D_HEADS, D_DV), lambda i, j: (layer, i, 0)),
            pl.BlockSpec((tm, tn), lambda i, j: (i, clamp(j, 0, OFF_DK // tn))),
            pl.BlockSpec((tm, tn), lambda i, j: (i, clamp(j, _J_V, D_V // tn))),
            pl.BlockSpec((tm, tn), lambda i, j: (i, clamp(j, _J_G, 2 * D_MODEL // tn))),
        ],
        out_shape=[jax.ShapeDtypeStruct((depth, batch, D_QK, seq), F32),
                   jax.ShapeDtypeStruct((depth, m * D_HEADS, D_DV), F32),
                   jax.ShapeDtypeStruct((m, OFF_DK), F32),
                   jax.ShapeDtypeStruct((m, D_V), F32),
                   jax.ShapeDtypeStruct((m, 2 * D_MODEL), F32)],
        scratch_shapes=[pltpu.VMEM((tm, D_MODEL), BF16)],
        input_output_aliases=aliases,
        compiler_params=_params(2),
        name=name,
    )(*args)


def _ret_body(*refs, chunk, nb, has_s0):
    if has_s0:
        q_ref, k_ref, v_ref, rg_ref, g_ref, s0_ref, s_out_ref, o_ref, s_scr = refs
    else:
        q_ref, k_ref, v_ref, rg_ref, g_ref, s_out_ref, o_ref, s_scr = refs
    c = pl.program_id(1)

    @pl.when(c == 0)
    def _():
        if has_s0:
            s_scr[...] = s0_ref[...]
        else:
            s_scr[...] = jnp.zeros_like(s_scr)

    ii = lax.broadcasted_iota(jnp.int32, (chunk, chunk), 0)
    jj = lax.broadcasted_iota(jnp.int32, (chunk, chunk), 1)
    causal = ii >= jj
    dist = jnp.where(causal, (ii - jj).astype(F32), 0.0)
    pos = lax.broadcasted_iota(jnp.int32, (chunk, 1), 0).astype(F32)
    g = g_ref[...]

    for h in range(R_HEADS):
        lg = LOG_GAMMAS[h]
        decay = jnp.where(causal, jnp.exp(lg * dist), 0.0)
        q_decay = jnp.exp(lg * (pos + 1.0))
        k_decay = jnp.exp(lg * (chunk - 1.0 - pos))
        s_decay = math.exp(lg * chunk)
        qk_cols = slice(h * R_DK, (h + 1) * R_DK)
        v_cols = slice(h * R_DV, (h + 1) * R_DV)
        for n in range(nb):
            rows = slice(n * chunk, (n + 1) * chunk)
            q = q_ref[rows, qk_cols]
            k = k_ref[rows, qk_cols] * (R_DK ** -0.5)
            vb = v_ref[rows, v_cols].astype(BF16)
            s_prev = s_scr[n, h]
            inner = _dot_nt(q.astype(BF16), k.astype(BF16)) * decay
            o = _dot(inner.astype(BF16), vb) + _dot((q * q_decay).astype(BF16), s_prev.astype(BF16))
            s_scr[n, h] = s_decay * s_prev + _dot_tn((k * k_decay).astype(BF16), vb)
            rg = rg_ref[rows, v_cols]
            o_ref[rows, v_cols] = (_head_rms(o, g) * (rg * jax.nn.sigmoid(rg))).astype(o_ref.dtype)

    @pl.when(c == pl.num_programs(1) - 1)
    def _():
        s_out_ref[...] = s_scr[...]


def _retention(pa, ret_g, s0, prev_states, *, layer, depth, batch, seq, chunk, nb, out_dtype, name):
    nc = seq // chunk
    rows = nb * chunk
    has_s0 = s0 is not None

    def row_idx(b, c):
        return b * nc + c

    in_specs = [
        pl.BlockSpec((rows, R_QK), lambda b, c: (row_idx(b, c), OFF_RQ // R_QK)),
        pl.BlockSpec((rows, R_QK), lambda b, c: (row_idx(b, c), OFF_RK // R_QK)),
        pl.BlockSpec((rows, R_V), lambda b, c: (row_idx(b, c), OFF_RV // R_V)),
        pl.BlockSpec((rows, R_V), lambda b, c: (row_idx(b, c), OFF_RG // R_V)),
        pl.BlockSpec((1, R_DV), lambda b, c: (0, 0)),
    ]
    args = [pa, pa, pa, pa, ret_g]
    state_spec = pl.BlockSpec((None, nb, R_HEADS, R_DK, R_DV), lambda b, c: (layer, b, 0, 0, 0))
    if has_s0:
        in_specs.append(state_spec)
        args.append(s0)
    states, o = _stacked_call(
        functools.partial(_ret_body, chunk=chunk, nb=nb, has_s0=has_s0),
        prev=prev_states, layer=layer, depth=depth, per_layer_shape=(batch, R_HEADS, R_DK, R_DV),
        stacked_spec=state_spec,
        grid=(batch // nb, nc),
        in_specs=in_specs,
        out_specs=[pl.BlockSpec((rows, R_V), lambda b, c: (row_idx(b, c), 0))],
        out_shape=[jax.ShapeDtypeStruct((batch * seq, R_V), out_dtype)],
        scratch_shapes=[pltpu.VMEM((nb, R_HEADS, R_DK, R_DV), F32)],
        compiler_params=_params(2),
        name=name,
    )(*args)
    return o, states


ATTN_COLS = 256
ALIBI_RADIX = 256


LOG2_E = math.log2(math.e)
ACC_PAD = 16


def _dattn_p_body(q_ref, kt_ref, v_ref, lam_ref, g_ref, o_ref, kb_scr, vt_scr, qs_scr, sa_scr, sb_scr, pt_scr, m_scr,
                  a_scr, acc_scr, *, tq, tk, seq, lam_init):
    h = pl.program_id(1)
    qi = pl.program_id(2)
    slope = _select_const(h, ALIBI_SLOPES)
    n_kchunks = seq // tk

    @pl.when(qi == 0)
    def _():
        lane = lax.broadcasted_iota(jnp.int32, (tk, 128), 1)
        ones_row = jnp.where(lax.broadcasted_iota(jnp.int32, (ACC_PAD, tk), 0) == 0, 1.0, 0.0).astype(BF16)
        for j in range(n_kchunks):
            rows = slice(j * tk, (j + 1) * tk)
            kpos = j * tk + lax.broadcasted_iota(jnp.int32, (tk, 128), 0)
            aug = jnp.where(lane < 3, kpos // ALIBI_RADIX, jnp.where(lane < 6, kpos % ALIBI_RADIX, 0))
            kb_scr[rows, 0:D_DV] = kt_ref[:, rows].T.astype(BF16)
            kb_scr[rows, D_DV:2 * D_DV] = aug.astype(F32).astype(BF16)
            vt_scr[j, 0:D_DV, :] = v_ref[rows, :].T.astype(BF16)
            vt_scr[j, D_DV:D_DV + ACC_PAD, :] = ones_row

    q = q_ref[...] * (D_DK ** -0.5 * LOG2_E)
    lane = lax.broadcasted_iota(jnp.int32, q.shape, 1)
    qs_scr[0:tq, 0:D_DV] = jnp.where(lane < D_DK, q, 0.0).astype(BF16)
    qs_scr[tq:2 * tq, 0:D_DV] = jnp.where(lane >= D_DK, q, 0.0).astype(BF16)
    c = jnp.full((2 * tq, 128), slope * LOG2_E, F32)
    c1 = c.astype(BF16).astype(F32)
    c2 = (c - c1).astype(BF16).astype(F32)
    c3 = (c - c1 - c2).astype(BF16).astype(F32)
    lane2 = lax.broadcasted_iota(jnp.int32, (2 * tq, 128), 1)
    piece = jnp.where((lane2 == 0) | (lane2 == 3), c1, jnp.where((lane2 == 1) | (lane2 == 4), c2, c3))
    qs_scr[:, D_DV:2 * D_DV] = jnp.where(lane2 < 3, piece * ALIBI_RADIX,
                                         jnp.where(lane2 < 6, piece, 0.0)).astype(BF16)
    m_scr[...] = jnp.full_like(m_scr, NEG_INF)
    acc_scr[...] = jnp.zeros_like(acc_scr)

    def scores(ki, s_scr):
        k = kb_scr[pl.ds(pl.multiple_of(ki * tk, tk), tk), :]
        for c in range(2 * tq // ATTN_COLS):
            cols = slice(c * ATTN_COLS, (c + 1) * ATTN_COLS)
            s_scr[:, cols] = _dot_nt(k, qs_scr[cols, :])

    def softmax_pv(ki, s_scr, diag):
        for c in range(2 * tq // 128):
            cols = slice(c * 128, (c + 1) * 128)
            q0 = (c * 128) % tq
            if diag is not None and q0 + 127 < diag * tk:
                pt_scr[:, cols] = jnp.zeros((tk, 128), BF16)
                a_scr[:, cols] = jnp.ones((1, 128), F32)
                continue
            s = s_scr[:, cols]
            if diag is not None and q0 < diag * tk + tk - 1:
                kloc = diag * tk + lax.broadcasted_iota(jnp.int32, s.shape, 0)
                qloc = q0 + lax.broadcasted_iota(jnp.int32, s.shape, 1)
                s = jnp.where(kloc <= qloc, s, NEG_INF)
            m_prev = m_scr[:, cols]
            m_new = jnp.maximum(m_prev, jnp.max(s, axis=0, keepdims=True))
            m_scr[:, cols] = m_new
            a_scr[:, cols] = jnp.exp2(m_prev - m_new)
            pt_scr[:, cols] = jnp.exp2(s - m_new).astype(BF16)
        acc_scr[...] = acc_scr[...] * a_scr[...] + _dot(vt_scr[ki], pt_scr[...])

    n_diag = tq // tk
    assert n_diag == 2
    scores(0, sa_scr)

    def chunk_pair(j, carry):
        scores(2 * j + 1, sb_scr)
        softmax_pv(2 * j, sa_scr, None)
        scores(2 * j + 2, sa_scr)
        softmax_pv(2 * j + 1, sb_scr, None)
        return carry

    lax.fori_loop(0, qi, chunk_pair, 0)
    n_full = qi * n_diag
    scores(n_full + 1, sb_scr)
    softmax_pv(n_full, sa_scr, 0)
    softmax_pv(n_full + 1, sb_scr, 1)

    lam = _diff_lambda(lam_ref, lam_init)
    o = acc_scr[0:D_DV, :] * (1.0 / acc_scr[D_DV:D_DV + 1, :])
    o = o[:, 0:tq] - lam * o[:, tq:2 * tq]
    g = jnp.concatenate([g_ref[...]] * (tq // 128), axis=1)
    o = o * lax.rsqrt(jnp.mean(o * o, axis=0, keepdims=True) + RMS_EPS) * g * (1.0 - lam_init)
    o_ref[...] = o.T.astype(o_ref.dtype)


def _diff_attn_prompt(pa, keys_t, vv, lam_p, diff_g, *, layer, batch, seq, lam_init, name):
    tq = min(seq, 512)
    tk = tq // 2
    nq = seq // tq
    assert all(s == 2.0 ** round(math.log2(s)) for s in ALIBI_SLOPES) and seq <= ALIBI_RADIX * ALIBI_RADIX
    g_tile = jnp.broadcast_to(diff_g.reshape(D_DV, 1), (D_DV, 128))
    return pl.pallas_call(
        functools.partial(_dattn_p_body, tq=tq, tk=tk, seq=seq, lam_init=lam_init),
        grid=(batch, D_HEADS, nq),
        in_specs=[
            pl.BlockSpec((tq, D_DV), lambda b, h, qi: (b * nq + qi, OFF_DQ // D_DV + h)),
            pl.BlockSpec((None, None, D_DV, seq), lambda b, h, qi: (layer, b, h, 0)),
            pl.BlockSpec((seq, D_DV), lambda b, h, qi: (b, h)),
            pl.BlockSpec((4, D_DK), lambda b, h, qi: (0, 0)),
            pl.BlockSpec((D_DV, 128), lambda b, h, qi: (0, 0)),
        ],
        out_specs=pl.BlockSpec((tq, D_DV), lambda b, h, qi: (b * nq + qi, h)),
        out_shape=jax.ShapeDtypeStruct((batch * seq, D_V), BF16),
        scratch_shapes=[
            pltpu.VMEM((seq, 2 * D_DV), BF16),
            pltpu.VMEM((seq // tk, D_DV + ACC_PAD, tk), BF16),
            pltpu.VMEM((2 * tq, 2 * D_DV), BF16),
            pltpu.VMEM((tk, 2 * tq), F32),
            pltpu.VMEM((tk, 2 * tq), F32),
            pltpu.VMEM((tk, 2 * tq), BF16),
            pltpu.VMEM((1, 2 * tq), F32),
            pltpu.VMEM((1, 2 * tq), F32),
            pltpu.VMEM((D_DV + ACC_PAD, 2 * tq), F32),
        ],
        compiler_params=_params(3),
        name=name,
    )(pa, keys_t, vv, lam_p, g_tile)


def _block_diag_rows(q, n_groups, group_width):
    t = q.shape[0]
    qt = jnp.concatenate([q] * n_groups, axis=0)
    rg = lax.broadcasted_iota(jnp.int32, qt.shape, 0) // t
    cg = lax.broadcasted_iota(jnp.int32, qt.shape, 1) // group_width
    return jnp.where(rg == cg, qt, 0.0)


PAGE_GROUP = 2


def _dattn_s_body(pt_ref, q_ref, kn_ref, vn_ref, *rest, n_pages, page, t_new, lam_init):
    k_refs = rest[:n_pages]
    v_refs = rest[n_pages:2 * n_pages]
    lam_ref, g_ref, o_ref = rest[2 * n_pages:]
    past = n_pages * page
    n_rows = 2 * D_HEADS * t_new

    qbd = _block_diag_rows(q_ref[...] * (D_DK ** -0.5), 2 * D_HEADS, D_DK).astype(BF16)
    r = lax.broadcasted_iota(jnp.int32, (n_rows, 1), 0)
    slope = jnp.exp2(-8.0 * ((r // (2 * t_new)) + 1).astype(F32) / D_HEADS)
    qpos = past + (r % t_new)

    def key_group(j0):
        return jnp.concatenate([k_refs[j][...].astype(BF16) for j in range(j0, j0 + PAGE_GROUP)], axis=1)

    s_past = jnp.concatenate([_dot(qbd, key_group(j0)) for j0 in range(0, n_pages, PAGE_GROUP)], axis=1)
    kpos = lax.broadcasted_iota(jnp.int32, (1, past), 1)
    s_past = s_past - slope * (qpos - kpos).astype(F32)

    pad = jnp.zeros((page - t_new, D_QK), F32)
    kn = jnp.concatenate([kn_ref[...], pad], axis=0).astype(BF16)
    vn = jnp.concatenate([vn_ref[...], pad], axis=0).astype(BF16)
    d_new = qpos - (past + lax.broadcasted_iota(jnp.int32, (1, page), 1))
    s_new = jnp.where(d_new >= 0, _dot_nt(qbd, kn) - slope * d_new.astype(F32), NEG_INF)

    m = jnp.maximum(jnp.max(s_past, axis=-1, keepdims=True), jnp.max(s_new, axis=-1, keepdims=True))
    p_past = jnp.exp(s_past - m)
    p_new = jnp.exp(s_new - m)
    denom = jnp.sum(p_past, axis=-1, keepdims=True) + jnp.sum(p_new, axis=-1, keepdims=True)
    acc = _dot(p_new.astype(BF16), vn)
    def value_page(j):
        return jnp.concatenate([v_refs[j][pl.ds(h, page, stride=D_HEADS), :] for h in range(D_HEADS)],
                               axis=1).astype(BF16)

    for j0 in range(0, n_pages, PAGE_GROUP):
        v_group = jnp.concatenate([value_page(j) for j in range(j0, j0 + PAGE_GROUP)], axis=0)
        acc = acc + _dot(p_past[:, j0 * page:(j0 + PAGE_GROUP) * page].astype(BF16), v_group)
    o = acc / denom

    lam = _diff_lambda(lam_ref, lam_init)
    g = g_ref[...]
    heads = []
    for h in range(D_HEADS):
        cols = slice(h * D_DV, (h + 1) * D_DV)
        o1 = o[(2 * h) * t_new:(2 * h + 1) * t_new, cols]
        o2 = o[(2 * h + 1) * t_new:(2 * h + 2) * t_new, cols]
        heads.append(_head_rms(o1 - lam * o2, g) * (1.0 - lam_init))
    o_ref[...] = jnp.concatenate(heads, axis=1)


def _diff_attn_sample(pa, kk, vv, cache_k, cache_v, page_table, lam_p, diff_g, *, layer, t_new, lam_init, name):
    db, n_pages = page_table.shape
    page = cache_k.shape[3]
    assert n_pages % PAGE_GROUP == 0
    pt = page_table.reshape(-1)

    def page_spec(j, shape):
        return pl.BlockSpec((None, None) + shape, lambda b, pt_ref: (layer, pt_ref[b * n_pages + j], 0, 0))

    grid_spec = pltpu.PrefetchScalarGridSpec(
        num_scalar_prefetch=1,
        grid=(db,),
        in_specs=[pl.BlockSpec((t_new, D_QK), lambda b, pt_ref: (b, OFF_DQ // D_QK)),
                  pl.BlockSpec((t_new, D_QK), lambda b, pt_ref: (b, 0)),
                  pl.BlockSpec((t_new, D_V), lambda b, pt_ref: (b, 0))]
        + [page_spec(j, (D_QK, page)) for j in range(n_pages)]
        + [page_spec(j, (page * D_HEADS, D_DV)) for j in range(n_pages)]
        + [pl.BlockSpec((4, D_DK), lambda b, pt_ref: (0, 0)),
           pl.BlockSpec((1, D_DV), lambda b, pt_ref: (0, 0))],
        out_specs=pl.BlockSpec((t_new, D_V), lambda b, pt_ref: (b, 0)),
    )
    return pl.pallas_call(
        functools.partial(_dattn_s_body, n_pages=n_pages, page=page, t_new=t_new, lam_init=lam_init),
        grid_spec=grid_spec,
        out_shape=jax.ShapeDtypeStruct((db * t_new, D_V), F32),
        compiler_params=_params(1),
        name=name,
    )(pt, pa, kk, vv, *([cache_k] * n_pages), *([cache_v] * n_pages), lam_p, diff_g)


def _mix_body(ro_ref, do_ref, gr_ref, gd_ref, x_ref, wr_ref, wd_ref, wo_ref, g_ref, b_ref, o_ref, *, alpha):
    ret_branch = _dot(ro_ref[...].astype(BF16), wr_ref[...])
    diff_branch = _dot(do_ref[...].astype(BF16), wd_ref[...])
    merged = jax.nn.sigmoid(gr_ref[...]) * ret_branch + jax.nn.sigmoid(gd_ref[...]) * diff_branch
    mix = _dot(merged.astype(BF16), wo_ref[...])
    o_ref[...] = _layer_norm(alpha * x_ref[...] + mix, g_ref[...], b_ref[...])


def _mixer_out(ro, do, gates, x, w_ret_o, w_diff_o, w_out, ln_g, ln_b, *, layer, alpha, name):
    m = x.shape[0]
    tm = min(m, 512)
    row = lambda i: (i, 0)
    wspec = pl.BlockSpec((None, D_MODEL, D_MODEL), lambda i: (layer, 0, 0))
    lnspec = pl.BlockSpec((None, None, 1, D_MODEL), lambda i: (layer, 0, 0, 0))
    return pl.pallas_call(
        functools.partial(_mix_body, alpha=alpha),
        grid=(m // tm,),
        in_specs=[pl.BlockSpec((tm, R_V), row), pl.BlockSpec((tm, D_V), row),
                  pl.BlockSpec((tm, D_MODEL), lambda i: (i, 0)), pl.BlockSpec((tm, D_MODEL), lambda i: (i, 1)),
                  pl.BlockSpec((tm, D_MODEL), row), wspec, wspec, wspec, lnspec, lnspec],
        out_specs=pl.BlockSpec((tm, D_MODEL), row),
        out_shape=jax.ShapeDtypeStruct((m, D_MODEL), F32),
        compiler_params=_params(1),
        name=name,
    )(ro, do, gates, gates, x, w_ret_o, w_diff_o, w_out, ln_g, ln_b)


def _softmax_rows(s):
    m = jnp.max(s, axis=-1, keepdims=True)
    e = jnp.exp(s - m)
    return e / jnp.sum(e, axis=-1, keepdims=True)


def _mem_p_body(x_ref, mk_ref, mv_ref, wq_ref, wo_ref, g_ref, b_ref, o_ref, *, alpha):
    x = x_ref[...]
    q = _dot(x.astype(BF16), wq_ref[...])
    heads = []
    for h in range(M_HEADS):
        cols = slice(h * M_DH, (h + 1) * M_DH)
        s = _dot_nt(q[:, cols].astype(BF16), mk_ref[:, cols].astype(BF16)) * (M_DH ** -0.5)
        heads.append(_dot(_softmax_rows(s).astype(BF16), mv_ref[:, cols].astype(BF16)))
    o = jnp.concatenate(heads, axis=1)
    att = _dot(o.astype(BF16), wo_ref[...])
    o_ref[...] = _layer_norm(alpha * x + att, g_ref[...], b_ref[...])


def _mem_attn_prompt(x, mk, mv, w_q, w_o, ln_g, ln_b, *, layer, seq, alpha, name):
    m = x.shape[0]
    mem_len = mk.shape[0] // (m // seq)
    tm = min(seq, 512)
    per_batch = seq // tm
    row = lambda i: (i, 0)
    wspec = pl.BlockSpec((None, D_MODEL, D_MODEL), lambda i: (layer, 0, 0))
    lnspec = pl.BlockSpec((None, None, 1, D_MODEL), lambda i: (layer, 1, 0, 0))
    mspec = pl.BlockSpec((mem_len, D_MODEL), lambda i: (i // per_batch, 0))
    return pl.pallas_call(
        functools.partial(_mem_p_body, alpha=alpha),
        grid=(m // tm,),
        in_specs=[pl.BlockSpec((tm, D_MODEL), row), mspec, mspec, wspec, wspec, lnspec, lnspec],
        out_specs=pl.BlockSpec((tm, D_MODEL), row),
        out_shape=jax.ShapeDtypeStruct((m, D_MODEL), F32),
        compiler_params=_params(1),
        name=name,
    )(x, mk, mv, w_q, w_o, ln_g, ln_b)


def _mem_s_body(x_ref, mk_ref, mv_ref, wq_ref, wo_ref, g_ref, b_ref, o_ref, *, alpha, n_samples, t_new):
    x = x_ref[...]
    q = _dot(x.astype(BF16), wq_ref[...])
    outs = []
    mem_len = mk_ref.shape[1] // (D_MODEL // 128)
    halves = M_DH // 128

    def mem_rows(ref, n):
        return jnp.concatenate(
            [ref[n, pl.ds(j * M_HEADS + h, mem_len, stride=M_HEADS * halves), :]
             for h in range(M_HEADS) for j in range(halves)], axis=1).astype(BF16)

    for n in range(n_samples):
        qbd = _block_diag_rows(q[n * t_new:(n + 1) * t_new, :], M_HEADS, M_DH).astype(BF16)
        s = _dot_nt(qbd, mem_rows(mk_ref, n)) * (M_DH ** -0.5)
        o = _dot(_softmax_rows(s).astype(BF16), mem_rows(mv_ref, n))
        outs.append(jnp.concatenate(
            [o[h * t_new:(h + 1) * t_new, h * M_DH:(h + 1) * M_DH] for h in range(M_HEADS)], axis=1))
    o = jnp.concatenate(outs, axis=0)
    att = _dot(o.astype(BF16), wo_ref[...])
    o_ref[...] = _layer_norm(alpha * x + att, g_ref[...], b_ref[...])


def _mem_attn_sample(x, cache_mk, cache_mv, w_q, w_o, ln_g, ln_b, *, layer, t_new, alpha, name):
    m = x.shape[0]
    db, mem_rows = cache_mk.shape[1:3]
    ns = min(db, 4)
    rows = ns * t_new
    row = lambda i: (i, 0)
    wspec = pl.BlockSpec((None, D_MODEL, D_MODEL), lambda i: (layer, 0, 0))
    lnspec = pl.BlockSpec((None, None, 1, D_MODEL), lambda i: (layer, 1, 0, 0))
    mspec = pl.BlockSpec((None, ns, mem_rows, 128), lambda i: (layer, i, 0, 0))
    return pl.pallas_call(
        functools.partial(_mem_s_body, alpha=alpha, n_samples=ns, t_new=t_new),
        grid=(db // ns,),
        in_specs=[pl.BlockSpec((rows, D_MODEL), row), mspec, mspec, wspec, wspec, lnspec, lnspec],
        out_specs=pl.BlockSpec((rows, D_MODEL), row),
        out_shape=jax.ShapeDtypeStruct((m, D_MODEL), F32),
        compiler_params=_params(1),
        name=name,
    )(x, cache_mk, cache_mv, w_q, w_o, ln_g, ln_b)


FF_CHUNK = 256


def _ffn_body(x_ref, wgu_ref, wd_ref, g_ref, b_ref, o_ref, *, alpha):
    x = x_ref[...]
    xb = x.astype(BF16)
    y = alpha * x
    for c in range(D_FF // FF_CHUNK):
        gate = _dot(xb, wgu_ref[:, c * FF_CHUNK:(c + 1) * FF_CHUNK])
        up = _dot(xb, wgu_ref[:, D_FF + c * FF_CHUNK:D_FF + (c + 1) * FF_CHUNK])
        hidden = (gate * jax.nn.sigmoid(gate) * up).astype(BF16)
        y = y + _dot(hidden, wd_ref[c * FF_CHUNK:(c + 1) * FF_CHUNK, :])
    o_ref[...] = _layer_norm(y, g_ref[...], b_ref[...])


def _ffn(x, w_gu, w_down, ln_g, ln_b, *, layer, alpha, name):
    m = x.shape[0]
    tm = min(m, 512)
    row = lambda i: (i, 0)
    lnspec = pl.BlockSpec((None, None, 1, D_MODEL), lambda i: (layer, 2, 0, 0))
    return pl.pallas_call(
        functools.partial(_ffn_body, alpha=alpha),
        grid=(m // tm,),
        in_specs=[pl.BlockSpec((tm, D_MODEL), row),
                  pl.BlockSpec((None, D_MODEL, 2 * D_FF), lambda i: (layer, 0, 0)),
                  pl.BlockSpec((None, D_FF, D_MODEL), lambda i: (layer, 0, 0)),
                  lnspec, lnspec],
        out_specs=pl.BlockSpec((tm, D_MODEL), row),
        out_shape=jax.ShapeDtypeStruct((m, D_MODEL), F32),
        compiler_params=_params(1),
        name=name,
    )(x, w_gu, w_down, ln_g, ln_b)


def kernel(x_prompt, x_sample, mem_prompt, cache_k, cache_v, state_ret, cache_mem_k, cache_mem_v, page_table, w_in, w_ret_o, w_diff_o, w_out, ret_norm_g, diff_norm_g, diff_lambda, w_mem_q, w_mem_kv, w_mem_o, w_ffn_gu, w_ffn_down, ln_g, ln_b):
    depth = w_in.shape[0]
    batch, seq, _ = x_prompt.shape
    db, t_new, _ = x_sample.shape
    mem_len = mem_prompt.shape[1]
    n_pool, page = cache_k.shape[1:3]
    alpha = (2 * depth) ** 0.25

    w_in, w_ret_o, w_diff_o, w_out, w_mem_q, w_mem_kv, w_mem_o, w_ffn_gu, w_ffn_down = (
        w.astype(BF16) for w in (w_in, w_ret_o, w_diff_o, w_out, w_mem_q, w_mem_kv, w_mem_o, w_ffn_gu, w_ffn_down))
    cache_k = jnp.transpose(cache_k, (0, 1, 3, 4, 5, 2)).reshape(depth, n_pool, D_QK, page)
    cache_v = cache_v.reshape(depth, n_pool, page * D_HEADS, D_DV)

    def mem_view(c):
        c = c.reshape(depth, db, mem_len, M_HEADS, M_DH // 128, 128)
        return jnp.transpose(c, (0, 1, 2, 4, 3, 5)).reshape(depth, db, mem_len * (D_MODEL // 128), 128)

    cache_mem_k = mem_view(cache_mem_k)
    cache_mem_v = mem_view(cache_mem_v)
    ln_g = ln_g.reshape(depth, 3, 1, D_MODEL)
    ln_b = ln_b.reshape(depth, 3, 1, D_MODEL)
    mem = mem_prompt.reshape(batch * mem_len, D_MODEL)
    xp = x_prompt.reshape(batch * seq, D_MODEL)
    xs = x_sample.reshape(db * t_new, D_MODEL)

    keys_t = values_flat = states_p = states_s = None
    outs = [[] for _ in range(4)]
    for l in range(depth):
        lam_init = 0.8 - 0.6 * math.exp(-0.3 * l)
        ret_g = ret_norm_g[l].reshape(1, R_DV)
        diff_g = diff_norm_g[l].reshape(1, D_DV)
        lam_p = diff_lambda[l]
        wk_t = w_in[l, :, OFF_DK:OFF_DV].T

        mk_p = _matmul(mem, w_mem_kv, l, 0, D_MODEL, f"mem_k_{l}")
        mv_p = _matmul(mem, w_mem_kv, l, D_MODEL, D_MODEL, f"mem_v_{l}")

        keys_t, values_flat, pa, vv, gates = _prompt_projection(
            xp, w_in, wk_t, keys_t, values_flat, layer=l, depth=depth, batch=batch, seq=seq, name=f"proj_p{l}")
        ro, states_p = _retention(pa, ret_g, None, states_p, layer=l, depth=depth, batch=batch, seq=seq,
                                  chunk=min(seq, 256), nb=1, out_dtype=BF16, name=f"retention_p{l}")
        do = _diff_attn_prompt(pa, keys_t, vv, lam_p, diff_g, layer=l, batch=batch, seq=seq, lam_init=lam_init,
                               name=f"diff_attn_p{l}")
        xp = _mixer_out(ro, do, gates, xp, w_ret_o, w_diff_o, w_out, ln_g, ln_b, layer=l, alpha=alpha,
                        name=f"mixer_out_p{l}")
        xp = _mem_attn_prompt(xp, mk_p, mv_p, w_mem_q, w_mem_o, ln_g, ln_b, layer=l, seq=seq, alpha=alpha,
                              name=f"mem_attn_p{l}")
        xp = _ffn(xp, w_ffn_gu, w_ffn_down, ln_g, ln_b, layer=l, alpha=alpha, name=f"ffn_p{l}")

        pa = _matmul(xs, w_in, l, 0, OFF_DK, f"proj_a_s{l}")
        k_s = _matmul(xs, w_in, l, OFF_DK, D_QK, f"proj_k_s{l}")
        v_s = _matmul(xs, w_in, l, OFF_DV, D_V, f"proj_v_s{l}")
        gates = _matmul(xs, w_in, l, OFF_GR, 2 * D_MODEL, f"proj_g_s{l}")
        ro, states_s = _retention(pa, ret_g, state_ret, states_s, layer=l, depth=depth, batch=db, seq=t_new,
                                  chunk=t_new, nb=min(db, 8), out_dtype=F32, name=f"retention_s{l}")
        do = _diff_attn_sample(pa, k_s, v_s, cache_k, cache_v, page_table, lam_p, diff_g, layer=l, t_new=t_new,
                               lam_init=lam_init, name=f"diff_attn_s{l}")
        xs = _mixer_out(ro, do, gates, xs, w_ret_o, w_diff_o, w_out, ln_g, ln_b, layer=l, alpha=alpha,
                        name=f"mixer_out_s{l}")
        xs = _mem_attn_sample(xs, cache_mem_k, cache_mem_v, w_mem_q, w_mem_o, ln_g, ln_b, layer=l, t_new=t_new,
                              alpha=alpha, name=f"mem_attn_s{l}")
        xs = _ffn(xs, w_ffn_gu, w_ffn_down, ln_g, ln_b, layer=l, alpha=alpha, name=f"ffn_s{l}")

        for lst, val in zip(outs, (
                k_s.reshape(db, t_new, D_HEADS, 2, D_DK), v_s.reshape(db, t_new, D_HEADS, D_DV),
                mk_p.reshape(batch, mem_len, M_HEADS, M_DH), mv_p.reshape(batch, mem_len, M_HEADS, M_DH))):
            lst.append(val)

    k_prompt = jnp.transpose(keys_t.reshape(depth, batch, D_HEADS, 2, D_DK, seq), (0, 1, 5, 2, 3, 4))
    v_prompt = values_flat.reshape(depth, batch, seq, D_HEADS, D_DV)
    k_sample, v_sample, mem_k, mem_v = (jnp.stack(o) for o in outs)
    return (xp.reshape(batch, seq, D_MODEL), xs.reshape(db, t_new, D_MODEL), k_prompt, v_prompt, k_sample, v_sample,
            states_p, states_s, mem_k, mem_v)
```

```python
import functools
import math

import jax
import jax.numpy as jnp
from jax import lax
from jax.experimental import pallas as pl
from jax.experimental.pallas import tpu as pltpu

F32 = jnp.float32
BF16 = jnp.bfloat16

D_MODEL = 1024
R_HEADS, R_DK, R_DV = 4, 128, 256
D_HEADS, D_DK = 8, 64
D_DV = 2 * D_DK
M_HEADS = 4
M_DH = D_MODEL // M_HEADS
D_FF = -(-8 * D_MODEL // (3 * 256)) * 256
LN_EPS = 1e-5
RMS_EPS = 1e-6
NEG_INF = -1e30

R_QK = R_HEADS * R_DK
R_V = R_HEADS * R_DV
D_QK = D_HEADS * 2 * D_DK
D_V = D_HEADS * D_DV
OFF_RQ = 0
OFF_RK = OFF_RQ + R_QK
OFF_RV = OFF_RK + R_QK
OFF_RG = OFF_RV + R_V
OFF_DQ = OFF_RG + R_V
OFF_DK = OFF_DQ + D_QK
OFF_DV = OFF_DK + D_QK
OFF_GR = OFF_DV + D_V
OFF_GD = OFF_GR + D_MODEL
IN_WIDTH = OFF_GD + D_MODEL

LOG_GAMMAS = tuple(math.log(1.0 - 2.0 ** (-5.0 - h)) for h in range(R_HEADS))
ALIBI_SLOPES = tuple(2.0 ** (-8.0 * (h + 1) / D_HEADS) for h in range(D_HEADS))

VMEM_LIMIT_BYTES = 56 * 1024 * 1024


def _params(n_grid_dims):
    return pltpu.CompilerParams(dimension_semantics=("arbitrary",) * n_grid_dims,
                                vmem_limit_bytes=VMEM_LIMIT_BYTES)


def _select_const(idx, values):
    out = jnp.float32(values[-1])
    for i in range(len(values) - 2, -1, -1):
        out = jnp.where(idx == i, jnp.float32(values[i]), out)
    return out


def _dot(a, b):
    return jnp.dot(a, b, preferred_element_type=F32)


def _dot_nt(a, b):
    return lax.dot_general(a, b, (((1,), (1,)), ((), ())), preferred_element_type=F32)


def _dot_tn(a, b):
    return lax.dot_general(a, b, (((0,), (0,)), ((), ())), preferred_element_type=F32)


def _layer_norm(y, g, b):
    mu = jnp.mean(y, axis=-1, keepdims=True)
    yc = y - mu
    var = jnp.mean(yc * yc, axis=-1, keepdims=True)
    return yc * lax.rsqrt(var + LN_EPS) * g + b


def _head_rms(o, g):
    return o * lax.rsqrt(jnp.mean(o * o, axis=-1, keepdims=True) + RMS_EPS) * g


def _diff_lambda(lam_ref, lam_init):
    lp = lam_ref[...]
    a = jnp.sum(lp[0:1, :] * lp[1:2, :], axis=-1, keepdims=True)
    b = jnp.sum(lp[2:3, :] * lp[3:4, :], axis=-1, keepdims=True)
    return jnp.exp(a) - jnp.exp(b) + lam_init


def _mm_body(x_ref, w_ref, o_ref, xb_ref):
    @pl.when(pl.program_id(1) == 0)
    def _():
        xb_ref[...] = x_ref[...].astype(BF16)

    o_ref[...] = _dot(xb_ref[...], w_ref[...])


def _matmul(x, w, layer, col0, ncols, name):
    m, k = x.shape
    tm = min(m, 2048)
    tn = min(ncols, 512)
    cb = col0 // tn
    return pl.pallas_call(
        _mm_body,
        grid=(m // tm, ncols // tn),
        in_specs=[pl.BlockSpec((tm, k), lambda i, j: (i, 0)),
                  pl.BlockSpec((None, k, tn), lambda i, j: (layer, 0, j + cb))],
        out_specs=pl.BlockSpec((tm, tn), lambda i, j: (i, j)),
        out_shape=jax.ShapeDtypeStruct((m, ncols), F32),
        scratch_shapes=[pltpu.VMEM((tm, k), BF16)],
        compiler_params=_params(2),
        name=name,
    )(x, w)


def _stacked_call(body, *, prev, layer, depth, per_layer_shape, stacked_spec, **kwargs):
    out_shape = kwargs.pop("out_shape")
    out_specs = kwargs.pop("out_specs")
    in_specs = kwargs.pop("in_specs")
    stacked = jax.ShapeDtypeStruct((depth,) + per_layer_shape, F32)
    aliases = {}
    extra = []
    if layer > 0:
        in_specs = list(in_specs) + [pl.BlockSpec(memory_space=pl.ANY)]
        aliases = {len(in_specs) - 1: 0}
        extra = [prev]

        def body_with_prev(*refs, _body=body, _n_in=len(in_specs)):
            return _body(*refs[:_n_in - 1], *refs[_n_in:])
        body = body_with_prev
    call = pl.pallas_call(body, in_specs=in_specs, out_specs=[stacked_spec] + list(out_specs),
                          out_shape=[stacked] + list(out_shape), input_output_aliases=aliases, **kwargs)
    return lambda *args: call(*args, *extra)


PROJ_TN = 512
_J_K = OFF_DK // PROJ_TN
_J_V = OFF_DV // PROJ_TN
_J_G = OFF_GR // PROJ_TN
_HEADS_PER_TILE = PROJ_TN // D_DV


def _proj_p_body(x_ref, w_ref, wkt_ref, kt_ref, vflat_ref, pa_ref, vv_ref, g_ref, xb_ref):
    j = pl.program_id(1)
    tm = x_ref.shape[0]

    @pl.when(j == 0)
    def _():
        xb_ref[...] = x_ref[...].astype(BF16)

    @pl.when(j < _J_K)
    def _():
        pa_ref[...] = _dot(xb_ref[...], w_ref[...]).astype(pa_ref.dtype)

    @pl.when((j >= _J_K) & (j < _J_V))
    def _():
        kt_ref[...] = _dot_nt(wkt_ref[...], xb_ref[...])

    @pl.when((j >= _J_V) & (j < _J_G))
    def _():
        v = _dot(xb_ref[...], w_ref[...])
        vv_ref[...] = v
        for hh in range(_HEADS_PER_TILE):
            head = (j - _J_V) * _HEADS_PER_TILE + hh
            vflat_ref[pl.ds(head, tm, stride=D_HEADS), :] = v[:, hh * D_DV:(hh + 1) * D_DV]

    @pl.when(j >= _J_G)
    def _():
        g_ref[...] = _dot(xb_ref[...], w_ref[...])


def _prompt_projection(x, w_in, wk_t, prev_keys_t, prev_vflat, *, layer, depth, batch, seq, name):
    m = x.shape[0]
    tm = min(seq, 1024)
    nsb = seq // tm
    tn = PROJ_TN
    n_j = IN_WIDTH // tn

    def clamp(j, lo, n):
        return jnp.clip(j - lo, 0, n - 1)

    in_specs = [pl.BlockSpec((tm, D_MODEL), lambda i, j: (i, 0)),
                pl.BlockSpec((None, D_MODEL, tn), lambda i, j: (layer, 0, jnp.where((j >= _J_K) & (j < _J_V), _J_K - 1, j))),
                pl.BlockSpec((tn, D_MODEL), lambda i, j: (clamp(j, _J_K, D_QK // tn), 0))]
    args = [x, w_in, wk_t]
    aliases = {}
    body = _proj_p_body
    if layer > 0:
        in_specs += [pl.BlockSpec(memory_space=pl.ANY)] * 2
        args += [prev_keys_t, prev_vflat]
        aliases = {3: 0, 4: 1}

        def body(x_ref, w_ref, wkt_ref, _prev_kt, _prev_vflat, *rest):
            return _proj_p_body(x_ref, w_ref, wkt_ref, *rest)
    return pl.pallas_call(
        body,
        grid=(m // tm, n_j),
        in_specs=in_specs,
        out_specs=[
            pl.BlockSpec((None, None, tn, tm), lambda i, j: (layer, i // nsb, clamp(j, _J_K, D_QK // tn), i % nsb)),
            pl.BlockSpec((None, tm * D_HEADS, D_DV), lambda i, j: (layer, i, 0)),
            pl.BlockSpec((tm, tn), lambda i, j: (i, clamp(j, 0, OFF_DK // tn))),
            pl.BlockSpec((tm, tn), lambda i, j: (i, clamp(j, _J_V, D_V // tn))),
            pl.BlockSpec((tm, tn), lambda i, j: (i, clamp(j, _J_G, 2 * D_MODEL // tn))),
        ],
        out_shape=[jax.ShapeDtypeStruct((depth, batch, D_QK, seq), F32),
                   jax.ShapeDtypeStruct((depth, m * D_HEADS, D_DV), F32),
                   jax.ShapeDtypeStruct((m, OFF_DK), BF16),
                   jax.ShapeDtypeStruct((m, D_V), F32),
                   jax.ShapeDtypeStruct((m, 2 * D_MODEL), F32)],
        scratch_shapes=[pltpu.VMEM((tm, D_MODEL), BF16)],
        input_output_aliases=aliases,
        compiler_params=_params(2),
        name=name,
    )(*args)


def _ret_body(*refs, chunk, nb, has_s0):
    if has_s0:
        q_ref, k_ref, v_ref, rg_ref, g_ref, s0_ref, s_out_ref, o_ref, s_scr = refs
    else:
        q_ref, k_ref, v_ref, rg_ref, g_ref, s_out_ref, o_ref, s_scr = refs
    c = pl.program_id(1)

    @pl.when(c == 0)
    def _():
        if has_s0:
            s_scr[...] = s0_ref[...]
        else:
            s_scr[...] = jnp.zeros_like(s_scr)

    ii = lax.broadcasted_iota(jnp.int32, (chunk, chunk), 0)
    jj = lax.broadcasted_iota(jnp.int32, (chunk, chunk), 1)
    causal = ii >= jj
    dist = jnp.where(causal, (ii - jj).astype(F32), 0.0)
    pos = lax.broadcasted_iota(jnp.int32, (chunk, 1), 0).astype(F32)
    g = g_ref[...]

    for h in range(R_HEADS):
        lg = LOG_GAMMAS[h]
        decay = jnp.where(causal, jnp.exp(lg * dist), 0.0)
        q_decay = jnp.exp(lg * (pos + 1.0))
        k_decay = jnp.exp(lg * (chunk - 1.0 - pos))
        s_decay = math.exp(lg * chunk)
        qk_cols = slice(h * R_DK, (h + 1) * R_DK)
        v_cols = slice(h * R_DV, (h + 1) * R_DV)
        for n in range(nb):
            rows = slice(n * chunk, (n + 1) * chunk)
            q = q_ref[rows, qk_cols].astype(F32)
            k = k_ref[rows, qk_cols].astype(F32) * (R_DK ** -0.5)
            vb = v_ref[rows, v_cols].astype(BF16)
            s_prev = s_scr[n, h]
            inner = _dot_nt(q.astype(BF16), k.astype(BF16)) * decay
            o = _dot(inner.astype(BF16), vb) + _dot((q * q_decay).astype(BF16), s_prev.astype(BF16))
            s_scr[n, h] = s_decay * s_prev + _dot_tn((k * k_decay).astype(BF16), vb)
            rg = rg_ref[rows, v_cols].astype(F32)
            o_ref[rows, v_cols] = (_head_rms(o, g) * (rg * jax.nn.sigmoid(rg))).astype(o_ref.dtype)

    @pl.when(c == pl.num_programs(1) - 1)
    def _():
        s_out_ref[...] = s_scr[...]


def _retention(pa, ret_g, s0, prev_states, *, layer, depth, batch, seq, chunk, nb, out_dtype, name):
    nc = seq // chunk
    rows = nb * chunk
    has_s0 = s0 is not None

    def row_idx(b, c):
        return b * nc + c

    in_specs = [
        pl.BlockSpec((rows, R_QK), lambda b, c: (row_idx(b, c), OFF_RQ // R_QK)),
        pl.BlockSpec((rows, R_QK), lambda b, c: (row_idx(b, c), OFF_RK // R_QK)),
        pl.BlockSpec((rows, R_V), lambda b, c: (row_idx(b, c), OFF_RV // R_V)),
        pl.BlockSpec((rows, R_V), lambda b, c: (row_idx(b, c), OFF_RG // R_V)),
        pl.BlockSpec((1, R_DV), lambda b, c: (0, 0)),
    ]
    args = [pa, pa, pa, pa, ret_g]
    state_spec = pl.BlockSpec((None, nb, R_HEADS, R_DK, R_DV), lambda b, c: (layer, b, 0, 0, 0))
    if has_s0:
        in_specs.append(state_spec)
        args.append(s0)
    states, o = _stacked_call(
        functools.partial(_ret_body, chunk=chunk, nb=nb, has_s0=has_s0),
        prev=prev_states, layer=layer, depth=depth, per_layer_shape=(batch, R_HEADS, R_DK, R_DV),
        stacked_spec=state_spec,
        grid=(batch // nb, nc),
        in_specs=in_specs,
        out_specs=[pl.BlockSpec((rows, R_V), lambda b, c: (row_idx(b, c), 0))],
        out_shape=[jax.ShapeDtypeStruct((batch * seq, R_V), out_dtype)],
        scratch_shapes=[pltpu.VMEM((nb, R_HEADS, R_DK, R_DV), F32)],
        compiler_params=_params(2),
        name=name,
    )(*args)
    return o, states


ATTN_COLS = 256
ALIBI_RADIX = 256


LOG2_E = math.log2(math.e)
ACC_PAD = 16


def _dattn_p_body(q_ref, kt_ref, v_ref, lam_ref, g_ref, o_ref, kb_scr, vt_scr, qs_scr, sa_scr, sb_scr, pt_scr, m_scr,
                  a_scr, acc_scr, *, tq, tk, seq, lam_init):
    h = pl.program_id(1)
    qi = pl.program_id(2)
    slope = _select_const(h, ALIBI_SLOPES)
    n_kchunks = seq // tk

    @pl.when(qi == 0)
    def _():
        lane = lax.broadcasted_iota(jnp.int32, (tk, 128), 1)
        ones_row = jnp.where(lax.broadcasted_iota(jnp.int32, (ACC_PAD, tk), 0) == 0, 1.0, 0.0).astype(BF16)
        for j in range(n_kchunks):
            rows = slice(j * tk, (j + 1) * tk)
            kpos = j * tk + lax.broadcasted_iota(jnp.int32, (tk, 128), 0)
            aug = jnp.where(lane < 3, kpos // ALIBI_RADIX, jnp.where(lane < 6, kpos % ALIBI_RADIX, 0))
            kb_scr[rows, 0:D_DV] = kt_ref[:, rows].T.astype(BF16)
            kb_scr[rows, D_DV:2 * D_DV] = aug.astype(F32).astype(BF16)
            vt_scr[j, 0:D_DV, :] = v_ref[rows, :].T.astype(BF16)
            vt_scr[j, D_DV:D_DV + ACC_PAD, :] = ones_row

    q = q_ref[...].astype(F32) * (D_DK ** -0.5 * LOG2_E)
    lane = lax.broadcasted_iota(jnp.int32, q.shape, 1)
    qs_scr[0:tq, 0:D_DV] = jnp.where(lane < D_DK, q, 0.0).astype(BF16)
    qs_scr[tq:2 * tq, 0:D_DV] = jnp.where(lane >= D_DK, q, 0.0).astype(BF16)
    c = jnp.full((2 * tq, 128), slope * LOG2_E, F32)
    c1 = c.astype(BF16).astype(F32)
    c2 = (c - c1).astype(BF16).astype(F32)
    c3 = (c - c1 - c2).astype(BF16).astype(F32)
    lane2 = lax.broadcasted_iota(jnp.int32, (2 * tq, 128), 1)
    piece = jnp.where((lane2 == 0) | (lane2 == 3), c1, jnp.where((lane2 == 1) | (lane2 == 4), c2, c3))
    qs_scr[:, D_DV:2 * D_DV] = jnp.where(lane2 < 3, piece * ALIBI_RADIX,
                                         jnp.where(lane2 < 6, piece, 0.0)).astype(BF16)
    m_scr[...] = jnp.full_like(m_scr, NEG_INF)
    acc_scr[...] = jnp.zeros_like(acc_scr)

    def scores(ki, s_scr):
        k = kb_scr[pl.ds(pl.multiple_of(ki * tk, tk), tk), :]
        for c in range(2 * tq // ATTN_COLS):
            cols = slice(c * ATTN_COLS, (c + 1) * ATTN_COLS)
            s_scr[:, cols] = _dot_nt(k, qs_scr[cols, :])

    def softmax_pv(ki, s_scr, diag):
        for c in range(2 * tq // 128):
            cols = slice(c * 128, (c + 1) * 128)
            q0 = (c * 128) % tq
            if diag is not None and q0 + 127 < diag * tk:
                pt_scr[:, cols] = jnp.zeros((tk, 128), BF16)
                a_scr[:, cols] = jnp.ones((1, 128), F32)
                continue
            s = s_scr[:, cols]
            if diag is not None and q0 < diag * tk + tk - 1:
                kloc = diag * tk + lax.broadcasted_iota(jnp.int32, s.shape, 0)
                qloc = q0 + lax.broadcasted_iota(jnp.int32, s.shape, 1)
                s = jnp.where(kloc <= qloc, s, NEG_INF)
            m_prev = m_scr[:, cols]
            m_new = jnp.maximum(m_prev, jnp.max(s, axis=0, keepdims=True))
            m_scr[:, cols] = m_new
            a_scr[:, cols] = jnp.exp2(m_prev - m_new)
            pt_scr[:, cols] = jnp.exp2(s - m_new).astype(BF16)
        acc_scr[...] = acc_scr[...] * a_scr[...] + _dot(vt_scr[ki], pt_scr[...])

    n_diag = tq // tk
    assert n_diag == 2
    scores(0, sa_scr)

    def chunk_pair(j, carry):
        scores(2 * j + 1, sb_scr)
        softmax_pv(2 * j, sa_scr, None)
        scores(2 * j + 2, sa_scr)
        softmax_pv(2 * j + 1, sb_scr, None)
        return carry

    lax.fori_loop(0, qi, chunk_pair, 0)
    n_full = qi * n_diag
    scores(n_full + 1, sb_scr)
    softmax_pv(n_full, sa_scr, 0)
    softmax_pv(n_full + 1, sb_scr, 1)

    lam = _diff_lambda(lam_ref, lam_init)
    o = acc_scr[0:D_DV, :] * (1.0 / acc_scr[D_DV:D_DV + 1, :])
    o = o[:, 0:tq] - lam * o[:, tq:2 * tq]
    g = jnp.concatenate([g_ref[...]] * (tq // 128), axis=1)
    o = o * lax.rsqrt(jnp.mean(o * o, axis=0, keepdims=True) + RMS_EPS) * g * (1.0 - lam_init)
    o_ref[...] = o.T.astype(o_ref.dtype)


def _diff_attn_prompt(pa, keys_t, vv, lam_p, diff_g, *, layer, batch, seq, lam_init, name):
    tq = min(seq, 512)
    tk = tq // 2
    nq = seq // tq
    assert all(s == 2.0 ** round(math.log2(s)) for s in ALIBI_SLOPES) and seq <= ALIBI_RADIX * ALIBI_RADIX
    g_tile = jnp.broadcast_to(diff_g.reshape(D_DV, 1), (D_DV, 128))
    return pl.pallas_call(
        functools.partial(_dattn_p_body, tq=tq, tk=tk, seq=seq, lam_init=lam_init),
        grid=(batch, D_HEADS, nq),
        in_specs=[
            pl.BlockSpec((tq, D_DV), lambda b, h, qi: (b * nq + qi, OFF_DQ // D_DV + h)),
            pl.BlockSpec((None, None, D_DV, seq), lambda b, h, qi: (layer, b, h, 0)),
            pl.BlockSpec((seq, D_DV), lambda b, h, qi: (b, h)),
            pl.BlockSpec((4, D_DK), lambda b, h, qi: (0, 0)),
            pl.BlockSpec((D_DV, 128), lambda b, h, qi: (0, 0)),
        ],
        out_specs=pl.BlockSpec((tq, D_DV), lambda b, h, qi: (b * nq + qi, h)),
        out_shape=jax.ShapeDtypeStruct((batch * seq, D_V), BF16),
        scratch_shapes=[
            pltpu.VMEM((seq, 2 * D_DV), BF16),
            pltpu.VMEM((seq // tk, D_DV + ACC_PAD, tk), BF16),
            pltpu.VMEM((2 * tq, 2 * D_DV), BF16),
            pltpu.VMEM((tk, 2 * tq), F32),
            pltpu.VMEM((tk, 2 * tq), F32),
            pltpu.VMEM((tk, 2 * tq), BF16),
            pltpu.VMEM((1, 2 * tq), F32),
            pltpu.VMEM((1, 2 * tq), F32),
            pltpu.VMEM((D_DV + ACC_PAD, 2 * tq), F32),
        ],
        compiler_params=_params(3),
        name=name,
    )(pa, keys_t, vv, lam_p, g_tile)


def _block_diag_rows(q, n_groups, group_width):
    t = q.shape[0]
    qt = jnp.concatenate([q] * n_groups, axis=0)
    rg = lax.broadcasted_iota(jnp.int32, qt.shape, 0) // t
    cg = lax.broadcasted_iota(jnp.int32, qt.shape, 1) // group_width
    return jnp.where(rg == cg, qt, 0.0)


PAGE_GROUP = 2


def _dattn_s_body(pt_ref, q_ref, kn_ref, vn_ref, *rest, n_pages, page, t_new, lam_init):
    k_refs = rest[:n_pages]
    v_refs = rest[n_pages:2 * n_pages]
    lam_ref, g_ref, o_ref = rest[2 * n_pages:]
    past = n_pages * page
    n_rows = 2 * D_HEADS * t_new

    qbd = _block_diag_rows(q_ref[...] * (D_DK ** -0.5), 2 * D_HEADS, D_DK).astype(BF16)
    r = lax.broadcasted_iota(jnp.int32, (n_rows, 1), 0)
    slope = jnp.exp2(-8.0 * ((r // (2 * t_new)) + 1).astype(F32) / D_HEADS)
    qpos = past + (r % t_new)

    def key_group(j0):
        return jnp.concatenate([k_refs[j][...].astype(BF16) for j in range(j0, j0 + PAGE_GROUP)], axis=1)

    s_past = jnp.concatenate([_dot(qbd, key_group(j0)) for j0 in range(0, n_pages, PAGE_GROUP)], axis=1)
    kpos = lax.broadcasted_iota(jnp.int32, (1, past), 1)
    s_past = s_past - slope * (qpos - kpos).astype(F32)

    pad = jnp.zeros((page - t_new, D_QK), F32)
    kn = jnp.concatenate([kn_ref[...], pad], axis=0).astype(BF16)
    vn = jnp.concatenate([vn_ref[...], pad], axis=0).astype(BF16)
    d_new = qpos - (past + lax.broadcasted_iota(jnp.int32, (1, page), 1))
    s_new = jnp.where(d_new >= 0, _dot_nt(qbd, kn) - slope * d_new.astype(F32), NEG_INF)

    m = jnp.maximum(jnp.max(s_past, axis=-1, keepdims=True), jnp.max(s_new, axis=-1, keepdims=True))
    p_past = jnp.exp(s_past - m)
    p_new = jnp.exp(s_new - m)
    denom = jnp.sum(p_past, axis=-1, keepdims=True) + jnp.sum(p_new, axis=-1, keepdims=True)
    acc = _dot(p_new.astype(BF16), vn)
    def value_page(j):
        return jnp.concatenate([v_refs[j][pl.ds(h, page, stride=D_HEADS), :] for h in range(D_HEADS)],
                               axis=1).astype(BF16)

    for j0 in range(0, n_pages, PAGE_GROUP):
        v_group = jnp.concatenate([value_page(j) for j in range(j0, j0 + PAGE_GROUP)], axis=0)
        acc = acc + _dot(p_past[:, j0 * page:(j0 + PAGE_GROUP) * page].astype(BF16), v_group)
    o = acc / denom

    lam = _diff_lambda(lam_ref, lam_init)
    g = g_ref[...]
    heads = []
    for h in range(D_HEADS):
        cols = slice(h * D_DV, (h + 1) * D_DV)
        o1 = o[(2 * h) * t_new:(2 * h + 1) * t_new, cols]
        o2 = o[(2 * h + 1) * t_new:(2 * h + 2) * t_new, cols]
        heads.append(_head_rms(o1 - lam * o2, g) * (1.0 - lam_init))
    o_ref[...] = jnp.concatenate(heads, axis=1)


def _diff_attn_sample(pa, kk, vv, cache_k, cache_v, page_table, lam_p, diff_g, *, layer, t_new, lam_init, name):
    db, n_pages = page_table.shape
    page = cache_k.shape[3]
    assert n_pages % PAGE_GROUP == 0
    pt = page_table.reshape(-1)

    def page_spec(j, shape):
        return pl.BlockSpec((None, None) + shape, lambda b, pt_ref: (layer, pt_ref[b * n_pages + j], 0, 0))

    grid_spec = pltpu.PrefetchScalarGridSpec(
        num_scalar_prefetch=1,
        grid=(db,),
        in_specs=[pl.BlockSpec((t_new, D_QK), lambda b, pt_ref: (b, OFF_DQ // D_QK)),
                  pl.BlockSpec((t_new, D_QK), lambda b, pt_ref: (b, 0)),
                  pl.BlockSpec((t_new, D_V), lambda b, pt_ref: (b, 0))]
        + [page_spec(j, (D_QK, page)) for j in range(n_pages)]
        + [page_spec(j, (page * D_HEADS, D_DV)) for j in range(n_pages)]
        + [pl.BlockSpec((4, D_DK), lambda b, pt_ref: (0, 0)),
           pl.BlockSpec((1, D_DV), lambda b, pt_ref: (0, 0))],
        out_specs=pl.BlockSpec((t_new, D_V), lambda b, pt_ref: (b, 0)),
    )
    return pl.pallas_call(
        functools.partial(_dattn_s_body, n_pages=n_pages, page=page, t_new=t_new, lam_init=lam_init),
        grid_spec=grid_spec,
        out_shape=jax.ShapeDtypeStruct((db * t_new, D_V), F32),
        compiler_params=_params(1),
        name=name,
    )(pt, pa, kk, vv, *([cache_k] * n_pages), *([cache_v] * n_pages), lam_p, diff_g)


def _mix_body(ro_ref, do_ref, gr_ref, gd_ref, x_ref, wr_ref, wd_ref, wo_ref, g_ref, b_ref, o_ref, *, alpha):
    ret_branch = _dot(ro_ref[...].astype(BF16), wr_ref[...])
    diff_branch = _dot(do_ref[...].astype(BF16), wd_ref[...])
    merged = jax.nn.sigmoid(gr_ref[...]) * ret_branch + jax.nn.sigmoid(gd_ref[...]) * diff_branch
    mix = _dot(merged.astype(BF16), wo_ref[...])
    o_ref[...] = _layer_norm(alpha * x_ref[...] + mix, g_ref[...], b_ref[...])


def _mixer_out(ro, do, gates, x, w_ret_o, w_diff_o, w_out, ln_g, ln_b, *, layer, alpha, name):
    m = x.shape[0]
    tm = min(m, 512)
    row = lambda i: (i, 0)
    wspec = pl.BlockSpec((None, D_MODEL, D_MODEL), lambda i: (layer, 0, 0))
    lnspec = pl.BlockSpec((None, None, 1, D_MODEL), lambda i: (layer, 0, 0, 0))
    return pl.pallas_call(
        functools.partial(_mix_body, alpha=alpha),
        grid=(m // tm,),
        in_specs=[pl.BlockSpec((tm, R_V), row), pl.BlockSpec((tm, D_V), row),
                  pl.BlockSpec((tm, D_MODEL), lambda i: (i, 0)), pl.BlockSpec((tm, D_MODEL), lambda i: (i, 1)),
                  pl.BlockSpec((tm, D_MODEL), row), wspec, wspec, wspec, lnspec, lnspec],
        out_specs=pl.BlockSpec((tm, D_MODEL), row),
        out_shape=jax.ShapeDtypeStruct((m, D_MODEL), F32),
        compiler_params=_params(1),
        name=name,
    )(ro, do, gates, gates, x, w_ret_o, w_diff_o, w_out, ln_g, ln_b)


def _softmax_rows(s):
    m = jnp.max(s, axis=-1, keepdims=True)
    e = jnp.exp(s - m)
    return e / jnp.sum(e, axis=-1, keepdims=True)


def _mem_p_body(x_ref, mk_ref, mv_ref, wq_ref, wo_ref, g_ref, b_ref, o_ref, *, alpha):
    x = x_ref[...]
    q = _dot(x.astype(BF16), wq_ref[...])
    heads = []
    for h in range(M_HEADS):
        cols = slice(h * M_DH, (h + 1) * M_DH)
        s = _dot_nt(q[:, cols].astype(BF16), mk_ref[:, cols].astype(BF16)) * (M_DH ** -0.5)
        heads.append(_dot(_softmax_rows(s).astype(BF16), mv_ref[:, cols].astype(BF16)))
    o = jnp.concatenate(heads, axis=1)
    att = _dot(o.astype(BF16), wo_ref[...])
    o_ref[...] = _layer_norm(alpha * x + att, g_ref[...], b_ref[...])


def _mem_attn_prompt(x, mk, mv, w_q, w_o, ln_g, ln_b, *, layer, seq, alpha, name):
    m = x.shape[0]
    mem_len = mk.shape[0] // (m // seq)
    tm = min(seq, 512)
    per_batch = seq // tm
    row = lambda i: (i, 0)
    wspec = pl.BlockSpec((None, D_MODEL, D_MODEL), lambda i: (layer, 0, 0))
    lnspec = pl.BlockSpec((None, None, 1, D_MODEL), lambda i: (layer, 1, 0, 0))
    mspec = pl.BlockSpec((mem_len, D_MODEL), lambda i: (i // per_batch, 0))
    return pl.pallas_call(
        functools.partial(_mem_p_body, alpha=alpha),
        grid=(m // tm,),
        in_specs=[pl.BlockSpec((tm, D_MODEL), row), mspec, mspec, wspec, wspec, lnspec, lnspec],
        out_specs=pl.BlockSpec((tm, D_MODEL), row),
        out_shape=jax.ShapeDtypeStruct((m, D_MODEL), F32),
        compiler_params=_params(1),
        name=name,
    )(x, mk, mv, w_q, w_o, ln_g, ln_b)


def _mem_s_body(x_ref, mk_ref, mv_ref, wq_ref, wo_ref, g_ref, b_ref, o_ref, *, alpha, n_samples, t_new):
    x = x_ref[...]
    q = _dot(x.astype(BF16), wq_ref[...])
    outs = []
    mem_len = mk_ref.shape[1] // (D_MODEL // 128)
    halves = M_DH // 128

    def mem_rows(ref, n):
        return jnp.concatenate(
            [ref[n, pl.ds(j * M_HEADS + h, mem_len, stride=M_HEADS * halves), :]
             for h in range(M_HEADS) for j in range(halves)], axis=1).astype(BF16)

    for n in range(n_samples):
        qbd = _block_diag_rows(q[n * t_new:(n + 1) * t_new, :], M_HEADS, M_DH).astype(BF16)
        s = _dot_nt(qbd, mem_rows(mk_ref, n)) * (M_DH ** -0.5)
        o = _dot(_softmax_rows(s).astype(BF16), mem_rows(mv_ref, n))
        outs.append(jnp.concatenate(
            [o[h * t_new:(h + 1) * t_new, h * M_DH:(h + 1) * M_DH] for h in range(M_HEADS)], axis=1))
    o = jnp.concatenate(outs, axis=0)
    att = _dot(o.astype(BF16), wo_ref[...])
    o_ref[...] = _layer_norm(alpha * x + att, g_ref[...], b_ref[...])


def _mem_attn_sample(x, cache_mk, cache_mv, w_q, w_o, ln_g, ln_b, *, layer, t_new, alpha, name):
    m = x.shape[0]
    db, mem_rows = cache_mk.shape[1:3]
    ns = min(db, 4)
    rows = ns * t_new
    row = lambda i: (i, 0)
    wspec = pl.BlockSpec((None, D_MODEL, D_MODEL), lambda i: (layer, 0, 0))
    lnspec = pl.BlockSpec((None, None, 1, D_MODEL), lambda i: (layer, 1, 0, 0))
    mspec = pl.BlockSpec((None, ns, mem_rows, 128), lambda i: (layer, i, 0, 0))
    return pl.pallas_call(
        functools.partial(_mem_s_body, alpha=alpha, n_samples=ns, t_new=t_new),
        grid=(db // ns,),
        in_specs=[pl.BlockSpec((rows, D_MODEL), row), mspec, mspec, wspec, wspec, lnspec, lnspec],
        out_specs=pl.BlockSpec((rows, D_MODEL), row),
        out_shape=jax.ShapeDtypeStruct((m, D_MODEL), F32),
        compiler_params=_params(1),
        name=name,
    )(x, cache_mk, cache_mv, w_q, w_o, ln_g, ln_b)


FF_CHUNK = 256


def _ffn_body(x_ref, wgu_ref, wd_ref, g_ref, b_ref, o_ref, *, alpha):
    x = x_ref[...]
    xb = x.astype(BF16)
    y = alpha * x
    for c in range(D_FF // FF_CHUNK):
        gate = _dot(xb, wgu_ref[:, c * FF_CHUNK:(c + 1) * FF_CHUNK])
        up = _dot(xb, wgu_ref[:, D_FF + c * FF_CHUNK:D_FF + (c + 1) * FF_CHUNK])
        hidden = (gate * jax.nn.sigmoid(gate) * up).astype(BF16)
        y = y + _dot(hidden, wd_ref[c * FF_CHUNK:(c + 1) * FF_CHUNK, :])
    o_ref[...] = _layer_norm(y, g_ref[...], b_ref[...])


def _ffn(x, w_gu, w_down, ln_g, ln_b, *, layer, alpha, name):
    m = x.shape[0]
    tm = min(m, 512)
    row = lambda i: (i, 0)
    lnspec = pl.BlockSpec((None, None, 1, D_MODEL), lambda i: (layer, 2, 0, 0))
    return pl.pallas_call(
        functools.partial(_ffn_body, alpha=alpha),
        grid=(m // tm,),
        in_specs=[pl.BlockSpec((tm, D_MODEL), row),
                  pl.BlockSpec((None, D_MODEL, 2 * D_FF), lambda i: (layer, 0, 0)),
                  pl.BlockSpec((None, D_FF, D_MODEL), lambda i: (layer, 0, 0)),
                  lnspec, lnspec],
        out_specs=pl.BlockSpec((tm, D_MODEL), row),
        out_shape=jax.ShapeDtypeStruct((m, D_MODEL), F32),
        compiler_params=_params(1),
        name=name,
    )(x, w_gu, w_down, ln_g, ln_b)


def kernel(x_prompt, x_sample, mem_prompt, cache_k, cache_v, state_ret, cache_mem_k, cache_mem_v, page_table, w_in, w_ret_o, w_diff_o, w_out, ret_norm_g, diff_norm_g, diff_lambda, w_mem_q, w_mem_kv, w_mem_o, w_ffn_gu, w_ffn_down, ln_g, ln_b):
    depth = w_in.shape[0]
    batch, seq, _ = x_prompt.shape
    db, t_new, _ = x_sample.shape
    mem_len = mem_prompt.shape[1]
    n_pool, page = cache_k.shape[1:3]
    alpha = (2 * depth) ** 0.25

    w_in, w_ret_o, w_diff_o, w_out, w_mem_q, w_mem_kv, w_mem_o, w_ffn_gu, w_ffn_down = (
        w.astype(BF16) for w in (w_in, w_ret_o, w_diff_o, w_out, w_mem_q, w_mem_kv, w_mem_o, w_ffn_gu, w_ffn_down))
    cache_k = jnp.transpose(cache_k, (0, 1, 3, 4, 5, 2)).reshape(depth, n_pool, D_QK, page)
    cache_v = cache_v.reshape(depth, n_pool, page * D_HEADS, D_DV)

    def mem_view(c):
        c = c.reshape(depth, db, mem_len, M_HEADS, M_DH // 128, 128)
        return jnp.transpose(c, (0, 1, 2, 4, 3, 5)).reshape(depth, db, mem_len * (D_MODEL // 128), 128)

    cache_mem_k = mem_view(cache_mem_k)
    cache_mem_v = mem_view(cache_mem_v)
    ln_g = ln_g.reshape(depth, 3, 1, D_MODEL)
    ln_b = ln_b.reshape(depth, 3, 1, D_MODEL)
    mem = mem_prompt.reshape(batch * mem_len, D_MODEL)
    xp = x_prompt.reshape(batch * seq, D_MODEL)
    xs = x_sample.reshape(db * t_new, D_MODEL)

    keys_t = values_flat = states_p = states_s = None
    outs = [[] for _ in range(4)]
    for l in range(depth):
        lam_init = 0.8 - 0.6 * math.exp(-0.3 * l)
        ret_g = ret_norm_g[l].reshape(1, R_DV)
        diff_g = diff_norm_g[l].reshape(1, D_DV)
        lam_p = diff_lambda[l]
        wk_t = w_in[l, :, OFF_DK:OFF_DV].T

        mk_p = _matmul(mem, w_mem_kv, l, 0, D_MODEL, f"mem_k_{l}")
        mv_p = _matmul(mem, w_mem_kv, l, D_MODEL, D_MODEL, f"mem_v_{l}")

        keys_t, values_flat, pa, vv, gates = _prompt_projection(
            xp, w_in, wk_t, keys_t, values_flat, layer=l, depth=depth, batch=batch, seq=seq, name=f"proj_p{l}")
        ro, states_p = _retention(pa, ret_g, None, states_p, layer=l, depth=depth, batch=batch, seq=seq,
                                  chunk=min(seq, 256), nb=1, out_dtype=BF16, name=f"retention_p{l}")
        do = _diff_attn_prompt(pa, keys_t, vv, lam_p, diff_g, layer=l, batch=batch, seq=seq, lam_init=lam_init,
                               name=f"diff_attn_p{l}")
        xp = _mixer_out(ro, do, gates, xp, w_ret_o, w_diff_o, w_out, ln_g, ln_b, layer=l, alpha=alpha,
                        name=f"mixer_out_p{l}")
        xp = _mem_attn_prompt(xp, mk_p, mv_p, w_mem_q, w_mem_o, ln_g, ln_b, layer=l, seq=seq, alpha=alpha,
                              name=f"mem_attn_p{l}")
        xp = _ffn(xp, w_ffn_gu, w_ffn_down, ln_g, ln_b, layer=l, alpha=alpha, name=f"ffn_p{l}")

        pa = _matmul(xs, w_in, l, 0, OFF_DK, f"proj_a_s{l}")
        k_s = _matmul(xs, w_in, l, OFF_DK, D_QK, f"proj_k_s{l}")
        v_s = _matmul(xs, w_in, l, OFF_DV, D_V, f"proj_v_s{l}")
        gates = _matmul(xs, w_in, l, OFF_GR, 2 * D_MODEL, f"proj_g_s{l}")
        ro, states_s = _retention(pa, ret_g, state_ret, states_s, layer=l, depth=depth, batch=db, seq=t_new,
                                  chunk=t_new, nb=min(db, 8), out_dtype=F32, name=f"retention_s{l}")
        do = _diff_attn_sample(pa, k_s, v_s, cache_k, cache_v, page_table, lam_p, diff_g, layer=l, t_new=t_new,
                               lam_init=lam_init, name=f"diff_attn_s{l}")
        xs = _mixer_out(ro, do, gates, xs, w_ret_o, w_diff_o, w_out, ln_g, ln_b, layer=l, alpha=alpha,
                        name=f"mixer_out_s{l}")
        xs = _mem_attn_sample(xs, cache_mem_k, cache_mem_v, w_mem_q, w_mem_o, ln_g, ln_b, layer=l, t_new=t_new,
                              alpha=alpha, name=f"mem_attn_s{l}")
        xs = _ffn(xs, w_ffn_gu, w_ffn_down, ln_g, ln_b, layer=l, alpha=alpha, name=f"ffn_s{l}")

        for lst, val in zip(outs, (
                k_s.reshape(db, t_new, D_HEADS, 2, D_DK), v_s.reshape(db, t_new, D_HEADS, D_DV),
                mk_p.reshape(batch, mem_len, M_HEADS, M_DH), mv_p.reshape(batch, mem_len, M_HEADS, M_DH))):
            lst.append(val)

    k_prompt = jnp.transpose(keys_t.reshape(depth, batch, D_HEADS, 2, D_DK, seq), (0, 1, 5, 2, 3, 4))
    v_prompt = values_flat.reshape(depth, batch, seq, D_HEADS, D_DV)
    k_sample, v_sample, mem_k, mem_v = (jnp.stack(o) for o in outs)
    return (xp.reshape(batch, seq, D_MODEL), xs.reshape(db, t_new, D_MODEL), k_prompt, v_prompt, k_sample, v_sample,
            states_p, states_s, mem_k, mem_v)
```
